```python
import math
import jax, jax.numpy as jnp
from jax import lax
import numpy as np

D_MODEL = 2048
BATCH = 4
SEQ = 2048
DEPTH = 2
DEC_BATCH = 128
DEC_SEQ = 8
PAST_LEN = 16384
PAGE_SIZE = 128

N_MIXERS = 2
N_CONV_LAYERS = (DEPTH + 1) // 2
N_MLSTM_LAYERS = DEPTH // 2
CONV_WIDTH = 31
CONV_STATE = CONV_WIDTH - 1
N_HEADS = 8
DV_HEAD = D_MODEL // N_HEADS
DQK_HEAD = DV_HEAD // 2
CHUNK = 64
D_FF = -(-8 * D_MODEL // (3 * 256)) * 256
RMS_EPS = 1e-6
LN_EPS = 1e-5

kernel_name = 'hybrid_conformer_conv_mlstm_decoder_step'


def _dense(x, w):
    return jnp.einsum('btd,de->bte', x, w)


def rms_norm(x, g):
    xf = x.astype(jnp.float32)
    y = xf * lax.rsqrt(jnp.mean(xf * xf, axis=-1, keepdims=True) + RMS_EPS)
    return (y * g.astype(jnp.float32)).astype(x.dtype)


def layer_norm(x, g, b):
    xf = x.astype(jnp.float32)
    xc = xf - jnp.mean(xf, axis=-1, keepdims=True)
    var = jnp.mean(xc * xc, axis=-1, keepdims=True)
    y = xc * lax.rsqrt(var + LN_EPS) * g.astype(jnp.float32) + b.astype(jnp.float32)
    return y.astype(x.dtype)


def conv_mixer(x, past_u, w_pw1, b_pw1, w_dw, b_dw, ln_g, ln_b, w_pw2, b_pw2):
    a, g = jnp.split(_dense(x, w_pw1) + b_pw1, 2, axis=-1)
    u = a * jax.nn.sigmoid(g)
    u_full = jnp.concatenate([past_u.astype(u.dtype), u], axis=1)
    y = lax.conv_general_dilated(
        u_full, w_dw[:, None, :], window_strides=(1,), padding='VALID',
        dimension_numbers=('NWC', 'WIO', 'NWC'), feature_group_count=D_MODEL) + b_dw
    y = jax.nn.silu(layer_norm(y, ln_g, ln_b))
    out = _dense(y, w_pw2) + b_pw2
    return out, u_full[:, -CONV_STATE:]


def mlstm_cell(q, k, v, log_i, log_f, C0, n0, m0):
    B, T = q.shape[0], q.shape[1]
    L = math.gcd(T, CHUNK)
    nc = T // L

    def to_chunks(a):
        a = a.reshape((B, nc, L) + a.shape[2:])
        return jnp.swapaxes(jnp.moveaxis(a, 1, 0), 2, 3)

    causal = jnp.tril(jnp.ones((L, L), dtype=bool))

    def step(carry, inp):
        C, n, m = carry
        qb, kb, vb, ib, fb = inp
        b = jnp.cumsum(fb, axis=-1)
        a_inter = b + m[..., None]
        d_intra = b[..., :, None] - b[..., None, :] + ib[..., None, :]
        d_intra = jnp.where(causal, d_intra, -jnp.inf)
        m_t = jnp.maximum(a_inter, jnp.max(d_intra, axis=-1))
        w_inter = jnp.exp(a_inter - m_t)
        s = jnp.einsum('bhtk,bhsk->bhts', qb, kb) * jnp.exp(d_intra - m_t[..., None])
        num = (w_inter[..., None] * jnp.einsum('bhtk,bhkv->bhtv', qb, C)
               + jnp.einsum('bhts,bhsv->bhtv', s, vb))
        den = w_inter * jnp.einsum('bhtk,bhk->bht', qb, n) + jnp.sum(s, axis=-1)
        h = num / jnp.maximum(jnp.abs(den), jnp.exp(-m_t))[..., None]
        m_new = m_t[..., -1]
        g_state = jnp.exp(b[..., -1] + m - m_new)
        g_rows = jnp.exp(b[..., -1:] - b + ib - m_new[..., None])
        kw = kb * g_rows[..., None]
        C_new = g_state[..., None, None] * C + jnp.einsum('bhsk,bhsv->bhkv', kw, vb)
        n_new = g_state[..., None] * n + jnp.sum(kw, axis=2)
        return (C_new, n_new, m_new), h

    (C, n, m), hs = lax.scan(step, (C0, n0, m0), tuple(map(to_chunks, (q, k, v, log_i, log_f))))
    h = jnp.moveaxis(jnp.swapaxes(hs, 2, 3), 0, 1).reshape(B, T, N_HEADS, DV_HEAD)
    return h, C, n, m


def mlstm_mixer(x, C0, n0, m0, w_q, w_k, w_v, w_o, w_i, b_i, w_f, b_f, g_mh, w_out):
    B, T, _ = x.shape
    f32 = jnp.float32
    q = _dense(x, w_q).reshape(B, T, N_HEADS, DQK_HEAD).astype(f32) * (DQK_HEAD ** -0.5)
    k = _dense(x, w_k).reshape(B, T, N_HEADS, DQK_HEAD).astype(f32)
    v = _dense(x, w_v).reshape(B, T, N_HEADS, DV_HEAD).astype(f32)
    log_i = (_dense(x, w_i) + b_i).astype(f32)
    log_f = jax.nn.log_sigmoid((_dense(x, w_f) + b_f).astype(f32))
    o = jax.nn.sigmoid(_dense(x, w_o))
    h, C, n, m = mlstm_cell(q, k, v, log_i, log_f,
                            C0.astype(f32), n0.astype(f32), m0.astype(f32))
    h = h * lax.rsqrt(jnp.mean(h * h, axis=-1, keepdims=True) + RMS_EPS)
    h = h.reshape(B, T, D_MODEL) * g_mh.astype(f32)
    out = _dense(o * h.astype(x.dtype), w_out)
    return out, C.astype(C0.dtype), n.astype(n0.dtype), m.astype(m0.dtype)


def swiglu_ffn(x, w_gate, w_up, w_down):
    return _dense(jax.nn.silu(_dense(x, w_gate)) * _dense(x, w_up), w_down)


def setup_inputs(seed: int = 0) -> dict:
    key = jax.random.key(seed)
    ks = iter(jax.random.split(key, 40))

    def nrm(shape, scale):
        return scale * jax.random.normal(next(ks), shape, jnp.float32)

    D = D_MODEL
    NC, NM = N_CONV_LAYERS, N_MLSTM_LAYERS
    HK = N_HEADS * DQK_HEAD
    HV = N_HEADS * DV_HEAD
    return {
        'x_prompt': nrm((BATCH, SEQ, D), 1.0),
        'x_sample': nrm((DEC_BATCH, DEC_SEQ, D), 1.0),
        'cache_conv': nrm((NC, DEC_BATCH, CONV_STATE, D), 0.5),
        'state_C': nrm((NM, DEC_BATCH, N_HEADS, DQK_HEAD, DV_HEAD), 0.3),
        'state_n': nrm((NM, DEC_BATCH, N_HEADS, DQK_HEAD), 0.3),
        'state_m': nrm((NM, DEC_BATCH, N_HEADS), 0.5),
        'conv_w_pw1': nrm((NC, D, 2 * D), D ** -0.5),
        'conv_b_pw1': nrm((NC, 2 * D), 0.02),
        'conv_w_dw': nrm((NC, CONV_WIDTH, D), CONV_WIDTH ** -0.5),
        'conv_b_dw': nrm((NC, D), 0.02),
        'conv_ln_g': 1.0 + nrm((NC, D), 0.05),
        'conv_ln_b': nrm((NC, D), 0.02),
        'conv_w_pw2': nrm((NC, D, D), D ** -0.5),
        'conv_b_pw2': nrm((NC, D), 0.02),
        'mlstm_w_q': nrm((NM, D, HK), D ** -0.5),
        'mlstm_w_k': nrm((NM, D, HK), D ** -0.5),
        'mlstm_w_v': nrm((NM, D, HV), D ** -0.5),
        'mlstm_w_o': nrm((NM, D, HV), D ** -0.5),
        'mlstm_w_i': nrm((NM, D, N_HEADS), D ** -0.5),
        'mlstm_b_i': -3.0 + nrm((NM, N_HEADS), 0.1),
        'mlstm_w_f': nrm((NM, D, N_HEADS), D ** -0.5),
        'mlstm_b_f': 3.0 + nrm((NM, N_HEADS), 0.1),
        'mlstm_g_norm': 1.0 + nrm((NM, HV), 0.05),
        'mlstm_w_out': nrm((NM, HV, D), HV ** -0.5),
        'ffn_w_gate': nrm((DEPTH, D, D_FF), D ** -0.5),
        'ffn_w_up': nrm((DEPTH, D, D_FF), D ** -0.5),
        'ffn_w_down': nrm((DEPTH, D_FF, D), D_FF ** -0.5),
        'norm_mix_pre': 1.0 + nrm((DEPTH, D), 0.05),
        'norm_mix_post': 1.0 + nrm((DEPTH, D), 0.05),
        'norm_ffn_pre': 1.0 + nrm((DEPTH, D), 0.05),
        'norm_ffn_post': 1.0 + nrm((DEPTH, D), 0.05),
    }


def reference(x_prompt, x_sample, cache_conv, state_C, state_n, state_m,
              conv_w_pw1, conv_b_pw1, conv_w_dw, conv_b_dw, conv_ln_g, conv_ln_b,
              conv_w_pw2, conv_b_pw2,
              mlstm_w_q, mlstm_w_k, mlstm_w_v, mlstm_w_o, mlstm_w_i, mlstm_b_i,
              mlstm_w_f, mlstm_b_f, mlstm_g_norm, mlstm_w_out,
              ffn_w_gate, ffn_w_up, ffn_w_down,
              norm_mix_pre, norm_mix_post, norm_ffn_pre, norm_ffn_post):
    conv_zero = jnp.zeros((BATCH, CONV_STATE, D_MODEL), cache_conv.dtype)
    C_zero = jnp.zeros((BATCH, N_HEADS, DQK_HEAD, DV_HEAD), state_C.dtype)
    n_zero = jnp.zeros((BATCH, N_HEADS, DQK_HEAD), state_n.dtype)
    m_zero = jnp.zeros((BATCH, N_HEADS), state_m.dtype)

    xs = [x_prompt, x_sample]
    conv_new = [[], []]
    C_new, n_new, m_new = [[], []], [[], []], [[], []]
    for i in range(DEPTH):
        j = i // N_MIXERS
        for gi in range(2):
            x = xs[gi]
            h = rms_norm(x, norm_mix_pre[i])
            if i % N_MIXERS == 0:
                past = conv_zero if gi == 0 else cache_conv[j]
                mix, buf = conv_mixer(h, past, conv_w_pw1[j], conv_b_pw1[j], conv_w_dw[j],
                                      conv_b_dw[j], conv_ln_g[j], conv_ln_b[j],
                                      conv_w_pw2[j], conv_b_pw2[j])
                conv_new[gi].append(buf)
            else:
                if gi == 0:
                    C0, n0, m0 = C_zero, n_zero, m_zero
                else:
                    C0, n0, m0 = state_C[j], state_n[j], state_m[j]
                mix, C1, n1, m1 = mlstm_mixer(h, C0, n0, m0, mlstm_w_q[j], mlstm_w_k[j],
                                              mlstm_w_v[j], mlstm_w_o[j], mlstm_w_i[j],
                                              mlstm_b_i[j], mlstm_w_f[j], mlstm_b_f[j],
                                              mlstm_g_norm[j], mlstm_w_out[j])
                C_new[gi].append(C1)
                n_new[gi].append(n1)
                m_new[gi].append(m1)
            x = x + rms_norm(mix, norm_mix_post[i])
            f = swiglu_ffn(rms_norm(x, norm_ffn_pre[i]), ffn_w_gate[i], ffn_w_up[i], ffn_w_down[i])
            xs[gi] = x + rms_norm(f, norm_ffn_post[i])

    return (xs[0], xs[1],
            jnp.stack(conv_new[0]), jnp.stack(conv_new[1]),
            jnp.stack(C_new[0]), jnp.stack(n_new[0]), jnp.stack(m_new[0]),
            jnp.stack(C_new[1]), jnp.stack(n_new[1]), jnp.stack(m_new[1]))
```

```python
import functools

import jax
import jax.numpy as jnp
from jax import lax
from jax.experimental import pallas as pl
from jax.experimental.pallas import tpu as pltpu

F32 = jnp.float32
BF16 = jnp.bfloat16

D_MODEL = 2048
N_HEADS = 8
DV_HEAD = D_MODEL // N_HEADS
DQK_HEAD = DV_HEAD // 2
CONV_WIDTH = 31
CONV_STATE = CONV_WIDTH - 1
RMS_EPS = 1e-6
LN_EPS = 1e-5

LANES = 128
SUBLANES = 8
BF16_ROWS = 16
MIB = 1024 * 1024
VMEM_LIMIT = 52 * MIB

ROW_TILE = 1024
ROW_TILE_FULL_N = 512
CONV_TIME_TILE = 256
CONV_HALO = 32
CONV_ROWS = 64
CONV_LANES = 256
CELL_CHUNK = 256
SAMPLE_PAD_T = BF16_ROWS
SAMPLE_SEQS = 4
SAMPLE_CONV_SEQS = 8


def _compiler_params(semantics):
    return pltpu.CompilerParams(dimension_semantics=semantics, vmem_limit_bytes=VMEM_LIMIT)


def _dot(a, b):
    return jnp.dot(a, b, preferred_element_type=F32)


def _rms(x, g):
    return x * lax.rsqrt(jnp.mean(x * x, axis=-1, keepdims=True) + RMS_EPS) * g


def _sigmoid(x):
    return jax.nn.sigmoid(x)


def _pw1_glu_kernel(x_ref, g_ref, wa_ref, wg_ref, ba_ref, bg_ref, u_ref, h_scr):
    @pl.when(pl.program_id(1) == 0)
    def _():
        h_scr[...] = _rms(x_ref[...], g_ref[...]).astype(BF16)

    h = h_scr[...]
    a = _dot(h, wa_ref[...]) + ba_ref[...]
    g = _dot(h, wg_ref[...]) + bg_ref[...]
    u_ref[...] = a * _sigmoid(g)


def _pw1_glu(x, g_pre, w, b):
    m_rows, d = x.shape
    tm, tn = ROW_TILE, 512
    nn = d // tn
    return pl.pallas_call(
        _pw1_glu_kernel,
        grid=(m_rows // tm, nn),
        in_specs=[
            pl.BlockSpec((tm, d), lambda m, n: (m, 0)),
            pl.BlockSpec((1, d), lambda m, n: (0, 0)),
            pl.BlockSpec((d, tn), lambda m, n: (0, n)),
            pl.BlockSpec((d, tn), lambda m, n: (0, n + nn)),
            pl.BlockSpec((1, tn), lambda m, n: (0, n)),
            pl.BlockSpec((1, tn), lambda m, n: (0, n + nn)),
        ],
        out_specs=pl.BlockSpec((tm, tn), lambda m, n: (m, n)),
        out_shape=jax.ShapeDtypeStruct((m_rows, d), F32),
        scratch_shapes=[pltpu.VMEM((tm, d), BF16)],
        compiler_params=_compiler_params(("parallel", "arbitrary")),
        name="pw1_glu",
    )(x, g_pre, w, w, b, b)


def _ln_silu(v, ln_g, ln_b):
    xc = v - jnp.mean(v, axis=-1, keepdims=True)
    var = jnp.mean(xc * xc, axis=-1, keepdims=True)
    y = xc * lax.rsqrt(var + LN_EPS) * ln_g + ln_b
    return y * _sigmoid(y)


def _conv_prompt_kernel(u_ref, w_ref, b_ref, lg_ref, lb_ref, y_ref, ext, cv):
    tt, d = u_ref.shape
    first_tap = CONV_HALO - CONV_STATE

    @pl.when(pl.program_id(1) == 0)
    def _():
        ext[0:CONV_HALO, :] = jnp.zeros((CONV_HALO, d), F32)

    @pl.when(pl.program_id(1) != 0)
    def _():
        ext[0:CONV_HALO, :] = ext[tt:tt + CONV_HALO, :]

    ext[CONV_HALO:CONV_HALO + tt, :] = u_ref[...]

    def lane_chunk(c, carry):
        lanes = pl.ds(pl.multiple_of(c * CONV_LANES, CONV_LANES), CONV_LANES)
        bias = b_ref[:, lanes]
        for r0 in range(0, tt, CONV_ROWS):
            acc = jnp.broadcast_to(bias, (CONV_ROWS, CONV_LANES))
            for j in range(CONV_WIDTH):
                acc = acc + w_ref[j:j + 1, lanes] * ext[pl.ds(r0 + first_tap + j, CONV_ROWS), lanes]
            cv[pl.ds(r0, CONV_ROWS), lanes] = acc
        return carry

    lax.fori_loop(0, d // CONV_LANES, lane_chunk, 0)
    y_ref[...] = _ln_silu(cv[...], lg_ref[...], lb_ref[...]).astype(BF16)


def _conv_prompt(u, batch, w_dw, b_dw, ln_g, ln_b):
    m_rows, d = u.shape
    tt = CONV_TIME_TILE
    nt = m_rows // batch // tt
    vec = pl.BlockSpec((1, d), lambda b, t: (0, 0))
    return pl.pallas_call(
        _conv_prompt_kernel,
        grid=(batch, nt),
        in_specs=[
            pl.BlockSpec((tt, d), lambda b, t: (b * nt + t, 0)),
            pl.BlockSpec((CONV_WIDTH, d), lambda b, t: (0, 0)),
            vec, vec, vec,
        ],
        out_specs=pl.BlockSpec((tt, d), lambda b, t: (b * nt + t, 0)),
        out_shape=jax.ShapeDtypeStruct((m_rows, d), BF16),
        scratch_shapes=[pltpu.VMEM((CONV_HALO + tt, d), F32), pltpu.VMEM((tt, d), F32)],
        compiler_params=_compiler_params(("parallel", "arbitrary")),
        name="conv_prompt",
    )(u, w_dw, b_dw, ln_g, ln_b)


def _conv_sample_kernel(uf_ref, w_ref, b_ref, lg_ref, lb_ref, y_ref, cv):
    nseq, _, d = uf_ref.shape
    t_new = y_ref.shape[0] // nseq
    cw = 512

    def lane_chunk(c, carry):
        lanes = pl.ds(pl.multiple_of(c * cw, cw), cw)
        bias = b_ref[:, lanes]
        for s in range(nseq):
            acc = jnp.broadcast_to(bias, (t_new, cw))
            for j in range(CONV_WIDTH):
                acc = acc + w_ref[j:j + 1, lanes] * uf_ref[s, pl.ds(j, t_new), lanes]
            cv[pl.ds(s * t_new, t_new), lanes] = acc
        return carry

    lax.fori_loop(0, d // cw, lane_chunk, 0)
    y_ref[...] = _ln_silu(cv[...], lg_ref[...], lb_ref[...]).astype(BF16)


def _conv_sample(u_full, w_dw, b_dw, ln_g, ln_b):
    nb, t_full, d = u_full.shape
    t_new = t_full - CONV_STATE
    nseq = SAMPLE_CONV_SEQS
    vec = pl.BlockSpec((1, d), lambda i: (0, 0))
    return pl.pallas_call(
        _conv_sample_kernel,
        grid=(nb // nseq,),
        in_specs=[
            pl.BlockSpec((nseq, t_full, d), lambda i: (i, 0, 0)),
            pl.BlockSpec((CONV_WIDTH, d), lambda i: (0, 0)),
            vec, vec, vec,
        ],
        out_specs=pl.BlockSpec((nseq * t_new, d), lambda i: (i, 0)),
        out_shape=jax.ShapeDtypeStruct((nb * t_new, d), BF16),
        scratch_shapes=[pltpu.VMEM((nseq * t_new, d), F32)],
        compiler_params=_compiler_params(("parallel",)),
        name="conv_sample",
    )(u_full, w_dw, b_dw, ln_g, ln_b)


def _proj_post_kernel(has_bias, *refs):
    if has_bias:
        a_ref, w_ref, b_ref, x_ref, gpost_ref, gnext_ref, xo_ref, ho_ref = refs
    else:
        a_ref, w_ref, x_ref, gpost_ref, gnext_ref, xo_ref, ho_ref = refs
    mix = _dot(a_ref[...], w_ref[...])
    if has_bias:
        mix = mix + b_ref[...]
    x1 = x_ref[...] + _rms(mix, gpost_ref[...])
    xo_ref[...] = x1
    ho_ref[...] = _rms(x1, gnext_ref[...]).astype(BF16)


def _proj_post(a, w, b, x, g_post, g_next):
    m_rows, k = a.shape
    d = w.shape[1]
    tm = ROW_TILE_FULL_N
    row = lambda width: pl.BlockSpec((tm, width), lambda m: (m, 0))
    vec = pl.BlockSpec((1, d), lambda m: (0, 0))
    has_bias = b is not None
    in_specs = [row(k), pl.BlockSpec((k, d), lambda m: (0, 0))]
    args = [a, w]
    if has_bias:
        in_specs.append(vec)
        args.append(b)
    in_specs += [row(d), vec, vec]
    args += [x, g_post, g_next]
    return pl.pallas_call(
        functools.partial(_proj_post_kernel, has_bias),
        grid=(m_rows // tm,),
        in_specs=in_specs,
        out_specs=[row(d), row(d)],
        out_shape=[jax.ShapeDtypeStruct((m_rows, d), F32), jax.ShapeDtypeStruct((m_rows, d), BF16)],
        compiler_params=_compiler_params(("parallel",)),
        name="proj_post",
    )(*args)


def _ffn_kernel(has_next, *refs):
    if has_next:
        h_ref, wg_ref, wu_ref, wd_ref, x_ref, gpost_ref, gnext_ref, xo_ref, ho_ref, acc = refs
    else:
        h_ref, wg_ref, wu_ref, wd_ref, x_ref, gpost_ref, xo_ref, acc = refs
    f = pl.program_id(1)
    h = h_ref[...]
    gate = _dot(h, wg_ref[...])
    up = _dot(h, wu_ref[...])
    part = _dot((gate * _sigmoid(gate) * up).astype(BF16), wd_ref[...])

    @pl.when(f == 0)
    def _():
        acc[...] = part

    @pl.when(f != 0)
    def _():
        acc[...] += part

    @pl.when(f == pl.num_programs(1) - 1)
    def _():
        x2 = x_ref[...] + _rms(acc[...], gpost_ref[...])
        xo_ref[...] = x2
        if has_next:
            ho_ref[...] = _rms(x2, gnext_ref[...]).astype(BF16)


def _ffn(h, w_gate, w_up, w_down, x, g_post, g_next):
    m_rows, d = h.shape
    d_ff = w_gate.shape[1]
    tm, tf = ROW_TILE_FULL_N, 512
    row = pl.BlockSpec((tm, d), lambda m, f: (m, 0))
    vec = pl.BlockSpec((1, d), lambda m, f: (0, 0))
    has_next = g_next is not None
    in_specs = [
        row,
        pl.BlockSpec((d, tf), lambda m, f: (0, f)),
        pl.BlockSpec((d, tf), lambda m, f: (0, f)),
        pl.BlockSpec((tf, d), lambda m, f: (f, 0)),
        row, vec,
    ]
    args = [h, w_gate, w_up, w_down, x, g_post]
    out_specs = [row]
    out_shape = [jax.ShapeDtypeStruct((m_rows, d), F32)]
    if has_next:
        in_specs.append(vec)
        args.append(g_next)
        out_specs.append(row)
        out_shape.append(jax.ShapeDtypeStruct((m_rows, d), BF16))
    outs = pl.pallas_call(
        functools.partial(_ffn_kernel, has_next),
        grid=(m_rows // tm, d_ff // tf),
        in_specs=in_specs,
        out_specs=out_specs,
        out_shape=out_shape,
        scratch_shapes=[pltpu.VMEM((tm, d), F32)],
        compiler_params=_compiler_params(("parallel", "arbitrary")),
        name="ffn",
    )(*args)
    return (outs[0], outs[1]) if has_next else (outs[0], None)


def _qkvo_kernel(h_ref, wq_ref, wk_ref, wv_ref, wo_ref, wif_ref, bif_ref,
                 q_ref, k_ref, v_ref, o_ref, g_ref):
    h = h_ref[...]
    q_ref[...] = (_dot(h, wq_ref[...]) * (DQK_HEAD ** -0.5)).astype(BF16)
    k_ref[...] = _dot(h, wk_ref[...]).astype(BF16)
    v_ref[...] = _dot(h, wv_ref[...]).astype(BF16)
    o_ref[...] = _sigmoid(_dot(h, wo_ref[...]))

    @pl.when(pl.program_id(1) == 0)
    def _():
        z = _dot(h, wif_ref[...]) + bif_ref[...]
        lane = lax.broadcasted_iota(jnp.int32, z.shape, 1)
        log_sig = jnp.minimum(z, 0.0) - jnp.log1p(jnp.exp(-jnp.abs(z)))
        g_ref[...] = jnp.where(lane < N_HEADS, z, log_sig)


def _qkvo(h, w_q, w_k, w_v, w_o, w_if, b_if):
    m_rows, d = h.shape
    hk, hv = w_q.shape[1], w_v.shape[1]
    tm, steps = ROW_TILE, 4
    col = lambda width: pl.BlockSpec((d, width), lambda m, n: (0, n))
    out = lambda width: pl.BlockSpec((tm, width), lambda m, n: (m, n))
    return pl.pallas_call(
        _qkvo_kernel,
        grid=(m_rows // tm, steps),
        in_specs=[
            pl.BlockSpec((tm, d), lambda m, n: (m, 0)),
            col(hk // steps), col(hk // steps), col(hv // steps), col(hv // steps),
            pl.BlockSpec((d, LANES), lambda m, n: (0, 0)),
            pl.BlockSpec((1, LANES), lambda m, n: (0, 0)),
        ],
        out_specs=[out(hk // steps), out(hk // steps), out(hv // steps), out(hv // steps),
                   pl.BlockSpec((tm, LANES), lambda m, n: (m, 0))],
        out_shape=[
            jax.ShapeDtypeStruct((m_rows, hk), BF16),
            jax.ShapeDtypeStruct((m_rows, hk), BF16),
            jax.ShapeDtypeStruct((m_rows, hv), BF16),
            jax.ShapeDtypeStruct((m_rows, hv), F32),
            jax.ShapeDtypeStruct((m_rows, LANES), F32),
        ],
        compiler_params=_compiler_params(("parallel", "arbitrary")),
        name="qkvo",
    )(h, w_q, w_k, w_v, w_o, w_if, b_if)


def _cell_chunk(q, k, v, log_i, log_f, c_state, n_state, m_state):
    length = q.shape[0]
    t_idx = lax.broadcasted_iota(jnp.int32, (length, length), 0)
    s_idx = lax.broadcasted_iota(jnp.int32, (length, length), 1)
    causal = s_idx <= t_idx
    eye = s_idx == t_idx
    b_col = jnp.sum(jnp.where(causal, log_f, 0.0), axis=1, keepdims=True)
    b_row = jnp.sum(jnp.where(eye, b_col, 0.0), axis=0, keepdims=True)
    i_col = jnp.sum(jnp.where(eye, log_i, 0.0), axis=1, keepdims=True)

    a_inter = b_col + m_state
    d_intra = jnp.where(causal, b_col - b_row + log_i, -jnp.inf)
    m_t = jnp.maximum(a_inter, jnp.max(d_intra, axis=1, keepdims=True))
    w_inter = jnp.exp(a_inter - m_t)
    qk = lax.dot_general(q, k, (((1,), (1,)), ((), ())), preferred_element_type=F32)
    s = qk * jnp.exp(d_intra - m_t)
    num = w_inter * _dot(q, c_state.astype(BF16)) + _dot(s.astype(BF16), v)
    den = (w_inter * jnp.sum(q.astype(F32) * n_state, axis=1, keepdims=True)
           + jnp.sum(s, axis=1, keepdims=True))
    h = num / jnp.maximum(jnp.abs(den), jnp.exp(-m_t))

    m_new = m_t[length - 1:length, :]
    b_last = b_col[length - 1:length, :]
    g_state = jnp.exp(b_last + m_state - m_new)
    g_rows = jnp.exp(b_last - b_col + i_col - m_new)
    kw = k.astype(F32) * g_rows
    c_new = g_state * c_state + lax.dot_general(
        kw.astype(BF16), v, (((0,), (0,)), ((), ())), preferred_element_type=F32)
    n_new = g_state * n_state + jnp.sum(kw, axis=0, keepdims=True)
    return h, c_new, n_new, m_new


def _head_norm_gate(h, o, g_mh):
    hn = h * lax.rsqrt(jnp.mean(h * h, axis=-1, keepdims=True) + RMS_EPS)
    return o * (hn * g_mh)


def _cell_prompt_kernel(q_ref, k_ref, v_ref, o_ref, gt_ref, gmh_ref,
                        hg_ref, c_out, n_out, m_out, c_scr, n_scr, m_scr):
    c = pl.program_id(1)

    @pl.when(c == 0)
    def _():
        c_scr[...] = jnp.zeros(c_scr.shape, F32)
        n_scr[...] = jnp.zeros(n_scr.shape, F32)
        m_scr[...] = jnp.zeros(m_scr.shape, F32)

    for hd in range(N_HEADS):
        qk_cols = slice(hd * DQK_HEAD, (hd + 1) * DQK_HEAD)
        v_cols = slice(hd * DV_HEAD, (hd + 1) * DV_HEAD)
        h, c_new, n_new, m_new = _cell_chunk(
            q_ref[:, qk_cols], k_ref[:, qk_cols], v_ref[:, v_cols],
            gt_ref[hd:hd + 1, :], gt_ref[N_HEADS + hd:N_HEADS + hd + 1, :],
            c_scr[hd], n_scr[hd:hd + 1, :], m_scr[hd:hd + 1, 0:1])
        c_scr[hd] = c_new
        n_scr[hd:hd + 1, :] = n_new
        m_scr[hd:hd + 1, :] = jnp.broadcast_to(m_new, (1, LANES))
        hg_ref[:, v_cols] = _head_norm_gate(h, o_ref[:, v_cols], gmh_ref[:, v_cols]).astype(BF16)

    @pl.when(c == pl.num_programs(1) - 1)
    def _():
        c_out[0] = c_scr[...]
        n_out[0] = n_scr[...]
        m_out[0] = m_scr[...]


def _cell_prompt(q, k, v, o, gates_t, g_mh, batch):
    m_rows, hk = q.shape
    hv = v.shape[1]
    length = CELL_CHUNK
    nc = m_rows // batch // length
    rows = lambda width: pl.BlockSpec((length, width), lambda b, c: (b * nc + c, 0))
    return pl.pallas_call(
        _cell_prompt_kernel,
        grid=(batch, nc),
        in_specs=[
            rows(hk), rows(hk), rows(hv), rows(hv),
            pl.BlockSpec((2 * N_HEADS, length), lambda b, c: (0, b * nc + c)),
            pl.BlockSpec((1, hv), lambda b, c: (0, 0)),
        ],
        out_specs=[
            rows(hv),
            pl.BlockSpec((1, N_HEADS, DQK_HEAD, DV_HEAD), lambda b, c: (b, 0, 0, 0)),
            pl.BlockSpec((1, N_HEADS, DQK_HEAD), lambda b, c: (b, 0, 0)),
            pl.BlockSpec((1, N_HEADS, LANES), lambda b, c: (b, 0, 0)),
        ],
        out_shape=[
            jax.ShapeDtypeStruct((m_rows, hv), BF16),
            jax.ShapeDtypeStruct((batch, N_HEADS, DQK_HEAD, DV_HEAD), F32),
            jax.ShapeDtypeStruct((batch, N_HEADS, DQK_HEAD), F32),
            jax.ShapeDtypeStruct((batch, N_HEADS, LANES), F32),
        ],
        scratch_shapes=[
            pltpu.VMEM((N_HEADS, DQK_HEAD, DV_HEAD), F32),
            pltpu.VMEM((N_HEADS, DQK_HEAD), F32),
            pltpu.VMEM((N_HEADS, LANES), F32),
        ],
        compiler_params=_compiler_params(("parallel", "arbitrary")),
        name="cell_prompt",
    )(q, k, v, o, gates_t, g_mh)


def _cell_sample_kernel(q_ref, k_ref, v_ref, o_ref, gt_ref, gmh_ref, c_in, n_in, m_in,
                        hg_ref, c_out, n_out, m_out):
    t_new = o_ref.shape[1]

    def one_sequence(b, carry):
        for hd in range(N_HEADS):
            qk_cols = slice(hd * DQK_HEAD, (hd + 1) * DQK_HEAD)
            v_cols = slice(hd * DV_HEAD, (hd + 1) * DV_HEAD)
            h, c_new, n_new, m_new = _cell_chunk(
                q_ref[b, :, qk_cols], k_ref[b, :, qk_cols], v_ref[b, :, v_cols],
                gt_ref[b, hd:hd + 1, :], gt_ref[b, N_HEADS + hd:N_HEADS + hd + 1, :],
                c_in[b, hd], n_in[b, hd:hd + 1, :], m_in[b, hd:hd + 1, 0:1])
            c_out[b, hd] = c_new
            n_out[b, hd:hd + 1, :] = n_new
            m_out[b, hd:hd + 1, :] = jnp.broadcast_to(m_new, (1, LANES))
            hg_ref[b, :, v_cols] = _head_norm_gate(
                h[0:t_new, :], o_ref[b, :, v_cols], gmh_ref[:, v_cols])
        return carry

    lax.fori_loop(0, q_ref.shape[0], one_sequence, 0)


def _cell_sample(q, k, v, o, gates_t, g_mh, c0, n0, m0):
    nb, tp, hk = q.shape
    hv = v.shape[2]
    t_new = o.shape[1]
    nseq = SAMPLE_SEQS
    blk3 = lambda s1, s2: pl.BlockSpec((nseq, s1, s2), lambda i: (i, 0, 0))
    c_spec = pl.BlockSpec((nseq, N_HEADS, DQK_HEAD, DV_HEAD), lambda i: (i, 0, 0, 0))
    return pl.pallas_call(
        _cell_sample_kernel,
        grid=(nb // nseq,),
        in_specs=[
            blk3(tp, hk), blk3(tp, hk), blk3(tp, hv), blk3(t_new, hv),
            blk3(2 * N_HEADS, tp),
            pl.BlockSpec((1, hv), lambda i: (0, 0)),
            c_spec, blk3(N_HEADS, DQK_HEAD), blk3(N_HEADS, LANES),
        ],
        out_specs=[blk3(t_new, hv), c_spec, blk3(N_HEADS, DQK_HEAD), blk3(N_HEADS, LANES)],
        out_shape=[
            jax.ShapeDtypeStruct((nb, t_new, hv), F32),
            jax.ShapeDtypeStruct((nb, N_HEADS, DQK_HEAD, DV_HEAD), F32),
            jax.ShapeDtypeStruct((nb, N_HEADS, DQK_HEAD), F32),
            jax.ShapeDtypeStruct((nb, N_HEADS, LANES), F32),
        ],
        compiler_params=_compiler_params(("parallel",)),
        name="cell_sample",
    )(q, k, v, o, gates_t, g_mh, c0, n0, m0)


def _row(v):
    return v.reshape(1, -1).astype(F32)


def _conv_layer(x, batch, past, p):
    d = x.shape[1]
    u = _pw1_glu(x, p["g_pre"], p["w_pw1"], p["b_pw1"])
    if past is None:
        y = _conv_prompt(u, batch, p["w_dw"], p["b_dw"], p["ln_g"], p["ln_b"])
        state = u.reshape(batch, -1, d)[:, -CONV_STATE:]
    else:
        u_full = jnp.concatenate([past.astype(F32), u.reshape(batch, -1, d)], axis=1)
        y = _conv_sample(u_full, p["w_dw"], p["b_dw"], p["ln_g"], p["ln_b"])
        state = u_full[:, -CONV_STATE:]
    return y, state


def _mlstm_layer(h, batch, state, p):
    m_rows = h.shape[0]
    t_len = m_rows // batch
    q, k, v, o, gates = _qkvo(h, p["w_q"], p["w_k"], p["w_v"], p["w_o"], p["w_if"], p["b_if"])
    gates = gates[:, :2 * N_HEADS]
    if state is None:
        hg, c_new, n_new, m_new = _cell_prompt(q, k, v, o, gates.T, p["g_mh"], batch)
    else:
        c0, n0, m0 = state
        pad_t = SAMPLE_PAD_T - t_len
        pad3 = lambda a: jnp.pad(a.reshape(batch, t_len, -1), ((0, 0), (0, pad_t), (0, 0)))
        g3 = gates.reshape(batch, t_len, 2 * N_HEADS)
        g_pad = jnp.concatenate(
            [jnp.full((batch, pad_t, N_HEADS), -jnp.inf, F32), jnp.zeros((batch, pad_t, N_HEADS), F32)], axis=2)
        gates_t = jnp.swapaxes(jnp.concatenate([g3, g_pad], axis=1), 1, 2)
        m0_lanes = jnp.broadcast_to(m0.astype(F32)[:, :, None], (batch, N_HEADS, LANES))
        hg, c_new, n_new, m_new = _cell_sample(
            pad3(q), pad3(k), pad3(v), o.reshape(batch, t_len, -1), gates_t, p["g_mh"],
            c0.astype(F32), n0.astype(F32), m0_lanes)
        hg = hg.reshape(m_rows, -1).astype(BF16)
    return hg, (c_new, n_new, m_new[:, :, 0])


def kernel(x_prompt, x_sample, cache_conv, state_C, state_n, state_m, conv_w_pw1, conv_b_pw1, conv_w_dw, conv_b_dw, conv_ln_g, conv_ln_b, conv_w_pw2, conv_b_pw2, mlstm_w_q, mlstm_w_k, mlstm_w_v, mlstm_w_o, mlstm_w_i, mlstm_b_i, mlstm_w_f, mlstm_b_f, mlstm_g_norm, mlstm_w_out, ffn_w_gate, ffn_w_up, ffn_w_down, norm_mix_pre, norm_mix_post, norm_ffn_pre, norm_ffn_post):
    depth = ffn_w_gate.shape[0]
    d = x_prompt.shape[-1]
    batches = (x_prompt.shape[0], x_sample.shape[0])
    xs = [x_prompt.reshape(-1, d), x_sample.reshape(-1, d)]
    hs = [None, None]
    conv_new = [[], []]
    c_new, n_new, m_new = [[], []], [[], []], [[], []]

    for i in range(depth):
        j = i // 2
        is_conv = i % 2 == 0
        if is_conv:
            p = {
                "g_pre": _row(norm_mix_pre[i]),
                "w_pw1": conv_w_pw1[j].astype(BF16), "b_pw1": _row(conv_b_pw1[j]),
                "w_dw": conv_w_dw[j].astype(F32), "b_dw": _row(conv_b_dw[j]),
                "ln_g": _row(conv_ln_g[j]), "ln_b": _row(conv_ln_b[j]),
            }
            w_proj, b_proj = conv_w_pw2[j].astype(BF16), _row(conv_b_pw2[j])
        else:
            w_if = jnp.concatenate([mlstm_w_i[j], mlstm_w_f[j]], axis=1)
            b_if = jnp.concatenate([mlstm_b_i[j], mlstm_b_f[j]], axis=0)
            pad = LANES - w_if.shape[1]
            p = {
                "w_q": mlstm_w_q[j].astype(BF16), "w_k": mlstm_w_k[j].astype(BF16),
                "w_v": mlstm_w_v[j].astype(BF16), "w_o": mlstm_w_o[j].astype(BF16),
                "w_if": jnp.pad(w_if, ((0, 0), (0, pad))).astype(BF16),
                "b_if": jnp.pad(_row(b_if), ((0, 0), (0, pad))),
                "g_mh": _row(mlstm_g_norm[j]),
            }
            w_proj, b_proj = mlstm_w_out[j].astype(BF16), None
        w_gate, w_up, w_down = (ffn_w_gate[i].astype(BF16), ffn_w_up[i].astype(BF16),
                                ffn_w_down[i].astype(BF16))
        next_is_mlstm = i + 1 < depth and (i + 1) % 2 == 1
        g_next = _row(norm_mix_pre[i + 1]) if next_is_mlstm else None

        for gi in range(2):
            x, batch = xs[gi], batches[gi]
            if is_conv:
                past = None if gi == 0 else cache_conv[j]
                mix_in, state = _conv_layer(x, batch, past, p)
                conv_new[gi].append(state.astype(cache_conv.dtype))
            else:
                state = None if gi == 0 else (state_C[j], state_n[j], state_m[j])
                mix_in, (c1, n1, m1) = _mlstm_layer(hs[gi], batch, state, p)
                c_new[gi].append(c1.astype(state_C.dtype))
                n_new[gi].append(n1.astype(state_n.dtype))
                m_new[gi].append(m1.astype(state_m.dtype))
            x, h_ffn = _proj_post(mix_in, w_proj, b_proj, x, _row(norm_mix_post[i]), _row(norm_ffn_pre[i]))
            xs[gi], hs[gi] = _ffn(h_ffn, w_gate, w_up, w_down, x, _row(norm_ffn_post[i]), g_next)

    return (xs[0].reshape(x_prompt.shape), xs[1].reshape(x_sample.shape),
            jnp.stack(conv_new[0]), jnp.stack(conv_new[1]),
            jnp.stack(c_new[0]), jnp.stack(n_new[0]), jnp.stack(m_new[0]),
            jnp.stack(c_new[1]), jnp.stack(n_new[1]), jnp.stack(m_new[1]))
```

```python
import functools

import jax
import jax.numpy as jnp
from jax import lax
from jax.experimental import pallas as pl
from jax.experimental.pallas import tpu as pltpu

F32 = jnp.float32
BF16 = jnp.bfloat16

D_MODEL = 2048
N_HEADS = 8
DV_HEAD = D_MODEL // N_HEADS
DQK_HEAD = DV_HEAD // 2
CONV_WIDTH = 31
CONV_STATE = CONV_WIDTH - 1
RMS_EPS = 1e-6
LN_EPS = 1e-5

LANES = 128
SUBLANES = 8
BF16_ROWS = 16
MIB = 1024 * 1024
VMEM_LIMIT = 52 * MIB

ROW_TILE = 1024
ROW_TILE_FULL_N = 512
CONV_TIME_TILE = 256
CONV_HALO = 32
CONV_ROW_STRIDE = 4
CELL_CHUNK = 256
SAMPLE_PAD_T = BF16_ROWS
SAMPLE_SEQS = 4
SAMPLE_CONV_SEQS = 8


def _compiler_params(semantics):
    return pltpu.CompilerParams(dimension_semantics=semantics, vmem_limit_bytes=VMEM_LIMIT)


def _dot(a, b):
    return jnp.dot(a, b, preferred_element_type=F32)


def _rms(x, g):
    return x * lax.rsqrt(jnp.mean(x * x, axis=-1, keepdims=True) + RMS_EPS) * g


def _sigmoid(x):
    return jax.nn.sigmoid(x)


def _pw1_glu_kernel(x_ref, g_ref, wa_ref, wg_ref, ba_ref, bg_ref, u_ref, h_scr):
    @pl.when(pl.program_id(1) == 0)
    def _():
        h_scr[...] = _rms(x_ref[...], g_ref[...]).astype(BF16)

    h = h_scr[...]
    a = _dot(h, wa_ref[...]) + ba_ref[...]
    g = _dot(h, wg_ref[...]) + bg_ref[...]
    u_ref[...] = a * _sigmoid(g)


def _pw1_glu(x, g_pre, w, b):
    m_rows, d = x.shape
    tm, tn = ROW_TILE, 512
    nn = d // tn
    return pl.pallas_call(
        _pw1_glu_kernel,
        grid=(m_rows // tm, nn),
        in_specs=[
            pl.BlockSpec((tm, d), lambda m, n: (m, 0)),
            pl.BlockSpec((1, d), lambda m, n: (0, 0)),
            pl.BlockSpec((d, tn), lambda m, n: (0, n)),
            pl.BlockSpec((d, tn), lambda m, n: (0, n + nn)),
            pl.BlockSpec((1, tn), lambda m, n: (0, n)),
            pl.BlockSpec((1, tn), lambda m, n: (0, n + nn)),
        ],
        out_specs=pl.BlockSpec((tm, tn), lambda m, n: (m, n)),
        out_shape=jax.ShapeDtypeStruct((m_rows, d), F32),
        scratch_shapes=[pltpu.VMEM((tm, d), BF16)],
        compiler_params=_compiler_params(("parallel", "arbitrary")),
        name="pw1_glu",
    )(x, g_pre, w, w, b, b)


def _ln_silu(v, ln_g, ln_b):
    xc = v - jnp.mean(v, axis=-1, keepdims=True)
    var = jnp.mean(xc * xc, axis=-1, keepdims=True)
    y = xc * lax.rsqrt(var + LN_EPS) * ln_g + ln_b
    return y * _sigmoid(y)


def _conv_prompt_kernel(u_ref, w_ref, b_ref, lg_ref, lb_ref, y_ref, ext, cv):
    tt, d = u_ref.shape
    nl = d // LANES
    first_tap = CONV_HALO - CONV_STATE
    block_rows = SUBLANES * CONV_ROW_STRIDE

    @pl.when(pl.program_id(1) == 0)
    def _():
        ext[:, 0:CONV_HALO, :] = jnp.zeros((nl, CONV_HALO, LANES), F32)

    @pl.when(pl.program_id(1) != 0)
    def _():
        ext[:, 0:CONV_HALO, :] = ext[:, tt:tt + CONV_HALO, :]

    for lt in range(nl):
        ext[lt, CONV_HALO:CONV_HALO + tt, :] = u_ref[:, lt * LANES:(lt + 1) * LANES]

    def lane_tile(lt, carry):
        lanes = pl.ds(pl.multiple_of(lt * LANES, LANES), LANES)
        bias = jnp.broadcast_to(b_ref[:, lanes], (SUBLANES, LANES))
        for r0 in range(0, tt, block_rows):
            accs = [bias] * CONV_ROW_STRIDE
            window = {}
            for j in range(CONV_WIDTH):
                wj = jnp.broadcast_to(w_ref[j:j + 1, lanes], (SUBLANES, LANES))
                for s in range(CONV_ROW_STRIDE):
                    if s + j not in window:
                        window[s + j] = ext[lt, pl.ds(r0 + first_tap + s + j, SUBLANES, stride=CONV_ROW_STRIDE), :]
                    accs[s] = accs[s] + wj * window[s + j]
            for s in range(CONV_ROW_STRIDE):
                cv[lt, pl.ds(r0 + s, SUBLANES, stride=CONV_ROW_STRIDE), :] = accs[s]
        return carry

    lax.fori_loop(0, nl, lane_tile, 0)

    v = cv[...]
    xc = v - jnp.sum(jnp.sum(v, axis=0), axis=-1, keepdims=True) / d
    var = jnp.sum(jnp.sum(xc * xc, axis=0), axis=-1, keepdims=True) / d
    inv = lax.rsqrt(var + LN_EPS)
    for lt in range(nl):
        cols = slice(lt * LANES, (lt + 1) * LANES)
        y = xc[lt] * inv * lg_ref[:, cols] + lb_ref[:, cols]
        y_ref[:, cols] = (y * _sigmoid(y)).astype(BF16)


def _conv_prompt(u, batch, w_dw, b_dw, ln_g, ln_b):
    m_rows, d = u.shape
    tt = CONV_TIME_TILE
    nt = m_rows // batch // tt
    vec = pl.BlockSpec((1, d), lambda b, t: (0, 0))
    return pl.pallas_call(
        _conv_prompt_kernel,
        grid=(batch, nt),
        in_specs=[
            pl.BlockSpec((tt, d), lambda b, t: (b * nt + t, 0)),
            pl.BlockSpec((CONV_WIDTH, d), lambda b, t: (0, 0)),
            vec, vec, vec,
        ],
        out_specs=pl.BlockSpec((tt, d), lambda b, t: (b * nt + t, 0)),
        out_shape=jax.ShapeDtypeStruct((m_rows, d), BF16),
        scratch_shapes=[pltpu.VMEM((d // LANES, CONV_HALO + tt, LANES), F32),
                        pltpu.VMEM((d // LANES, tt, LANES), F32)],
        compiler_params=_compiler_params(("parallel", "arbitrary")),
        name="conv_prompt",
    )(u, w_dw, b_dw, ln_g, ln_b)


def _conv_sample_kernel(uf_ref, w_ref, b_ref, lg_ref, lb_ref, y_ref, cv):
    nseq, _, d = uf_ref.shape
    t_new = y_ref.shape[0] // nseq
    cw = 512

    def lane_chunk(c, carry):
        lanes = pl.ds(pl.multiple_of(c * cw, cw), cw)
        bias = b_ref[:, lanes]
        for s in range(nseq):
            acc = jnp.broadcast_to(bias, (t_new, cw))
            for j in range(CONV_WIDTH):
                acc = acc + w_ref[j:j + 1, lanes] * uf_ref[s, pl.ds(j, t_new), lanes]
            cv[pl.ds(s * t_new, t_new), lanes] = acc
        return carry

    lax.fori_loop(0, d // cw, lane_chunk, 0)
    y_ref[...] = _ln_silu(cv[...], lg_ref[...], lb_ref[...]).astype(BF16)


def _conv_sample(u_full, w_dw, b_dw, ln_g, ln_b):
    nb, t_full, d = u_full.shape
    t_new = t_full - CONV_STATE
    nseq = SAMPLE_CONV_SEQS
    vec = pl.BlockSpec((1, d), lambda i: (0, 0))
    return pl.pallas_call(
        _conv_sample_kernel,
        grid=(nb // nseq,),
        in_specs=[
            pl.BlockSpec((nseq, t_full, d), lambda i: (i, 0, 0)),
            pl.BlockSpec((CONV_WIDTH, d), lambda i: (0, 0)),
            vec, vec, vec,
        ],
        out_specs=pl.BlockSpec((nseq * t_new, d), lambda i: (i, 0)),
        out_shape=jax.ShapeDtypeStruct((nb * t_new, d), BF16),
        scratch_shapes=[pltpu.VMEM((nseq * t_new, d), F32)],
        compiler_params=_compiler_params(("parallel",)),
        name="conv_sample",
    )(u_full, w_dw, b_dw, ln_g, ln_b)


def _proj_post_kernel(has_bias, *refs):
    if has_bias:
        a_ref, w_ref, b_ref, x_ref, gpost_ref, gnext_ref, xo_ref, ho_ref = refs
    else:
        a_ref, w_ref, x_ref, gpost_ref, gnext_ref, xo_ref, ho_ref = refs
    mix = _dot(a_ref[...], w_ref[...])
    if has_bias:
        mix = mix + b_ref[...]
    x1 = x_ref[...] + _rms(mix, gpost_ref[...])
    xo_ref[...] = x1
    ho_ref[...] = _rms(x1, gnext_ref[...]).astype(BF16)


def _proj_post(a, w, b, x, g_post, g_next):
    m_rows, k = a.shape
    d = w.shape[1]
    tm = ROW_TILE_FULL_N
    row = lambda width: pl.BlockSpec((tm, width), lambda m: (m, 0))
    vec = pl.BlockSpec((1, d), lambda m: (0, 0))
    has_bias = b is not None
    in_specs = [row(k), pl.BlockSpec((k, d), lambda m: (0, 0))]
    args = [a, w]
    if has_bias:
        in_specs.append(vec)
        args.append(b)
    in_specs += [row(d), vec, vec]
    args += [x, g_post, g_next]
    return pl.pallas_call(
        functools.partial(_proj_post_kernel, has_bias),
        grid=(m_rows // tm,),
        in_specs=in_specs,
        out_specs=[row(d), row(d)],
        out_shape=[jax.ShapeDtypeStruct((m_rows, d), F32), jax.ShapeDtypeStruct((m_rows, d), BF16)],
        compiler_params=_compiler_params(("parallel",)),
        name="proj_post",
    )(*args)


def _ffn_kernel(has_next, *refs):
    if has_next:
        h_ref, wg_ref, wu_ref, wd_ref, x_ref, gpost_ref, gnext_ref, xo_ref, ho_ref, acc = refs
    else:
        h_ref, wg_ref, wu_ref, wd_ref, x_ref, gpost_ref, xo_ref, acc = refs
    f = pl.program_id(1)

    @pl.when(f == 0)
    def _():
        acc[...] = jnp.zeros(acc.shape, F32)

    h = h_ref[...]
    gate = _dot(h, wg_ref[...])
    up = _dot(h, wu_ref[...])
    acc[...] += _dot((gate * _sigmoid(gate) * up).astype(BF16), wd_ref[...])

    @pl.when(f == pl.num_programs(1) - 1)
    def _():
        x2 = x_ref[...] + _rms(acc[...], gpost_ref[...])
        xo_ref[...] = x2
        if has_next:
            ho_ref[...] = _rms(x2, gnext_ref[...]).astype(BF16)


def _ffn(h, w_gate, w_up, w_down, x, g_post, g_next):
    m_rows, d = h.shape
    d_ff = w_gate.shape[1]
    tm, tf = ROW_TILE_FULL_N, 512
    row = pl.BlockSpec((tm, d), lambda m, f: (m, 0))
    vec = pl.BlockSpec((1, d), lambda m, f: (0, 0))
    has_next = g_next is not None
    in_specs = [
        row,
        pl.BlockSpec((d, tf), lambda m, f: (0, f)),
        pl.BlockSpec((d, tf), lambda m, f: (0, f)),
        pl.BlockSpec((tf, d), lambda m, f: (f, 0)),
        row, vec,
    ]
    args = [h, w_gate, w_up, w_down, x, g_post]
    out_specs = [row]
    out_shape = [jax.ShapeDtypeStruct((m_rows, d), F32)]
    if has_next:
        in_specs.append(vec)
        args.append(g_next)
        out_specs.append(row)
        out_shape.append(jax.ShapeDtypeStruct((m_rows, d), BF16))
    outs = pl.pallas_call(
        functools.partial(_ffn_kernel, has_next),
        grid=(m_rows // tm, d_ff // tf),
        in_specs=in_specs,
        out_specs=out_specs,
        out_shape=out_shape,
        scratch_shapes=[pltpu.VMEM((tm, d), F32)],
        compiler_params=_compiler_params(("parallel", "arbitrary")),
        name="ffn",
    )(*args)
    return (outs[0], outs[1]) if has_next else (outs[0], None)


def _qkvo_kernel(h_ref, wq_ref, wk_ref, wv_ref, wo_ref, wif_ref, bif_ref,
                 q_ref, k_ref, v_ref, o_ref, g_ref):
    h = h_ref[...]
    q_ref[...] = (_dot(h, wq_ref[...]) * (DQK_HEAD ** -0.5)).astype(BF16)
    k_ref[...] = _dot(h, wk_ref[...]).astype(BF16)
    v_ref[...] = _dot(h, wv_ref[...]).astype(BF16)
    o_ref[...] = _sigmoid(_dot(h, wo_ref[...]))

    @pl.when(pl.program_id(1) == 0)
    def _():
        z = _dot(h, wif_ref[...]) + bif_ref[...]
        lane = lax.broadcasted_iota(jnp.int32, z.shape, 1)
        log_sig = jnp.minimum(z, 0.0) - jnp.log1p(jnp.exp(-jnp.abs(z)))
        g_ref[...] = jnp.where(lane < N_HEADS, z, log_sig)


def _qkvo(h, w_q, w_k, w_v, w_o, w_if, b_if):
    m_rows, d = h.shape
    hk, hv = w_q.shape[1], w_v.shape[1]
    tm, steps = ROW_TILE, 4
    col = lambda width: pl.BlockSpec((d, width), lambda m, n: (0, n))
    out = lambda width: pl.BlockSpec((tm, width), lambda m, n: (m, n))
    return pl.pallas_call(
        _qkvo_kernel,
        grid=(m_rows // tm, steps),
        in_specs=[
            pl.BlockSpec((tm, d), lambda m, n: (m, 0)),
            col(hk // steps), col(hk // steps), col(hv // steps), col(hv // steps),
            pl.BlockSpec((d, LANES), lambda m, n: (0, 0)),
            pl.BlockSpec((1, LANES), lambda m, n: (0, 0)),
        ],
        out_specs=[out(hk // steps), out(hk // steps), out(hv // steps), out(hv // steps),
                   pl.BlockSpec((tm, LANES), lambda m, n: (m, 0))],
        out_shape=[
            jax.ShapeDtypeStruct((m_rows, hk), BF16),
            jax.ShapeDtypeStruct((m_rows, hk), BF16),
            jax.ShapeDtypeStruct((m_rows, hv), BF16),
            jax.ShapeDtypeStruct((m_rows, hv), F32),
            jax.ShapeDtypeStruct((m_rows, LANES), F32),
        ],
        compiler_params=_compiler_params(("parallel", "arbitrary")),
        name="qkvo",
    )(h, w_q, w_k, w_v, w_o, w_if, b_if)


def _cell_chunk(q, k, v, log_i, log_f, c_state, n_state, m_state):
    length = q.shape[0]
    t_idx = lax.broadcasted_iota(jnp.int32, (length, length), 0)
    s_idx = lax.broadcasted_iota(jnp.int32, (length, length), 1)
    causal = s_idx <= t_idx
    eye = s_idx == t_idx
    b_col = jnp.sum(jnp.where(causal, log_f, 0.0), axis=1, keepdims=True)
    b_row = jnp.sum(jnp.where(eye, b_col, 0.0), axis=0, keepdims=True)
    i_col = jnp.sum(jnp.where(eye, log_i, 0.0), axis=1, keepdims=True)

    a_inter = b_col + m_state
    d_intra = jnp.where(causal, b_col - b_row + log_i, -jnp.inf)
    m_t = jnp.maximum(a_inter, jnp.max(d_intra, axis=1, keepdims=True))
    w_inter = jnp.exp(a_inter - m_t)
    qk = lax.dot_general(q, k, (((1,), (1,)), ((), ())), preferred_element_type=F32)
    s = qk * jnp.exp(d_intra - m_t)
    num = w_inter * _dot(q, c_state.astype(BF16)) + _dot(s.astype(BF16), v)
    den = (w_inter * jnp.sum(q.astype(F32) * n_state, axis=1, keepdims=True)
           + jnp.sum(s, axis=1, keepdims=True))
    h = num / jnp.maximum(jnp.abs(den), jnp.exp(-m_t))

    m_new = m_t[length - 1:length, :]
    b_last = b_col[length - 1:length, :]
    g_state = jnp.exp(b_last + m_state - m_new)
    g_rows = jnp.exp(b_last - b_col + i_col - m_new)
    kw = k.astype(F32) * g_rows
    c_new = g_state * c_state + lax.dot_general(
        kw.astype(BF16), v, (((0,), (0,)), ((), ())), preferred_element_type=F32)
    n_new = g_state * n_state + jnp.sum(kw, axis=0, keepdims=True)
    return h, c_new, n_new, m_new


def _head_norm_gate(h, o, g_mh):
    hn = h * lax.rsqrt(jnp.mean(h * h, axis=-1, keepdims=True) + RMS_EPS)
    return o * (hn * g_mh)


def _cell_prompt_kernel(q_ref, k_ref, v_ref, o_ref, gt_ref, gmh_ref,
                        hg_ref, c_out, n_out, m_out, c_scr, n_scr, m_scr):
    c = pl.program_id(1)

    @pl.when(c == 0)
    def _():
        c_scr[...] = jnp.zeros(c_scr.shape, F32)
        n_scr[...] = jnp.zeros(n_scr.shape, F32)
        m_scr[...] = jnp.zeros(m_scr.shape, F32)

    for hd in range(N_HEADS):
        qk_cols = slice(hd * DQK_HEAD, (hd + 1) * DQK_HEAD)
        v_cols = slice(hd * DV_HEAD, (hd + 1) * DV_HEAD)
        h, c_new, n_new, m_new = _cell_chunk(
            q_ref[:, qk_cols], k_ref[:, qk_cols], v_ref[:, v_cols],
            gt_ref[hd:hd + 1, :], gt_ref[N_HEADS + hd:N_HEADS + hd + 1, :],
            c_scr[hd], n_scr[hd:hd + 1, :], m_scr[hd:hd + 1, 0:1])
        c_scr[hd] = c_new
        n_scr[hd:hd + 1, :] = n_new
        m_scr[hd:hd + 1, :] = jnp.broadcast_to(m_new, (1, LANES))
        hg_ref[:, v_cols] = _head_norm_gate(h, o_ref[:, v_cols], gmh_ref[:, v_cols]).astype(BF16)

    @pl.when(c == pl.num_programs(1) - 1)
    def _():
        c_out[0] = c_scr[...]
        n_out[0] = n_scr[...]
        m_out[0] = m_scr[...]


def _cell_prompt(q, k, v, o, gates_t, g_mh, batch):
    m_rows, hk = q.shape
    hv = v.shape[1]
    length = CELL_CHUNK
    nc = m_rows // batch // length
    rows = lambda width: pl.BlockSpec((length, width), lambda b, c: (b * nc + c, 0))
    return pl.pallas_call(
        _cell_prompt_kernel,
        grid=(batch, nc),
        in_specs=[
            rows(hk), rows(hk), rows(hv), rows(hv),
            pl.BlockSpec((2 * N_HEADS, length), lambda b, c: (0, b * nc + c)),
            pl.BlockSpec((1, hv), lambda b, c: (0, 0)),
        ],
        out_specs=[
            rows(hv),
            pl.BlockSpec((1, N_HEADS, DQK_HEAD, DV_HEAD), lambda b, c: (b, 0, 0, 0)),
            pl.BlockSpec((1, N_HEADS, DQK_HEAD), lambda b, c: (b, 0, 0)),
            pl.BlockSpec((1, N_HEADS, LANES), lambda b, c: (b, 0, 0)),
        ],
        out_shape=[
            jax.ShapeDtypeStruct((m_rows, hv), BF16),
            jax.ShapeDtypeStruct((batch, N_HEADS, DQK_HEAD, DV_HEAD), F32),
            jax.ShapeDtypeStruct((batch, N_HEADS, DQK_HEAD), F32),
            jax.ShapeDtypeStruct((batch, N_HEADS, LANES), F32),
        ],
        scratch_shapes=[
            pltpu.VMEM((N_HEADS, DQK_HEAD, DV_HEAD), F32),
            pltpu.VMEM((N_HEADS, DQK_HEAD), F32),
            pltpu.VMEM((N_HEADS, LANES), F32),
        ],
        compiler_params=_compiler_params(("parallel", "arbitrary")),
        name="cell_prompt",
    )(q, k, v, o, gates_t, g_mh)


def _cell_sample_kernel(q_ref, k_ref, v_ref, o_ref, gt_ref, gmh_ref, c_in, n_in, m_in,
                        hg_ref, c_out, n_out, m_out):
    t_new = o_ref.shape[1]

    def one_sequence(b, carry):
        for hd in range(N_HEADS):
            qk_cols = slice(hd * DQK_HEAD, (hd + 1) * DQK_HEAD)
            v_cols = slice(hd * DV_HEAD, (hd + 1) * DV_HEAD)
            h, c_new, n_new, m_new = _cell_chunk(
                q_ref[b, :, qk_cols], k_ref[b, :, qk_cols], v_ref[b, :, v_cols],
                gt_ref[b, hd:hd + 1, :], gt_ref[b, N_HEADS + hd:N_HEADS + hd + 1, :],
                c_in[b, hd], n_in[b, hd:hd + 1, :], m_in[b, hd:hd + 1, 0:1])
            c_out[b, hd] = c_new
            n_out[b, hd:hd + 1, :] = n_new
            m_out[b, hd:hd + 1, :] = jnp.broadcast_to(m_new, (1, LANES))
            hg_ref[b, :, v_cols] = _head_norm_gate(
                h[0:t_new, :], o_ref[b, :, v_cols], gmh_ref[:, v_cols])
        return carry

    lax.fori_loop(0, q_ref.shape[0], one_sequence, 0)


def _cell_sample(q, k, v, o, gates_t, g_mh, c0, n0, m0):
    nb, tp, hk = q.shape
    hv = v.shape[2]
    t_new = o.shape[1]
    nseq = SAMPLE_SEQS
    blk3 = lambda s1, s2: pl.BlockSpec((nseq, s1, s2), lambda i: (i, 0, 0))
    c_spec = pl.BlockSpec((nseq, N_HEADS, DQK_HEAD, DV_HEAD), lambda i: (i, 0, 0, 0))
    return pl.pallas_call(
        _cell_sample_kernel,
        grid=(nb // nseq,),
        in_specs=[
            blk3(tp, hk), blk3(tp, hk), blk3(tp, hv), blk3(t_new, hv),
            blk3(2 * N_HEADS, tp),
            pl.BlockSpec((1, hv), lambda i: (0, 0)),
            c_spec, blk3(N_HEADS, DQK_HEAD), blk3(N_HEADS, LANES),
        ],
        out_specs=[blk3(t_new, hv), c_spec, blk3(N_HEADS, DQK_HEAD), blk3(N_HEADS, LANES)],
        out_shape=[
            jax.ShapeDtypeStruct((nb, t_new, hv), F32),
            jax.ShapeDtypeStruct((nb, N_HEADS, DQK_HEAD, DV_HEAD), F32),
            jax.ShapeDtypeStruct((nb, N_HEADS, DQK_HEAD), F32),
            jax.ShapeDtypeStruct((nb, N_HEADS, LANES), F32),
        ],
        compiler_params=_compiler_params(("parallel",)),
        name="cell_sample",
    )(q, k, v, o, gates_t, g_mh, c0, n0, m0)


def _row(v):
    return v.reshape(1, -1).astype(F32)


def _conv_layer(x, batch, past, p):
    d = x.shape[1]
    u = _pw1_glu(x, p["g_pre"], p["w_pw1"], p["b_pw1"])
    if past is None:
        y = _conv_prompt(u, batch, p["w_dw"], p["b_dw"], p["ln_g"], p["ln_b"])
        state = u.reshape(batch, -1, d)[:, -CONV_STATE:]
    else:
        u_full = jnp.concatenate([past.astype(F32), u.reshape(batch, -1, d)], axis=1)
        y = _conv_sample(u_full, p["w_dw"], p["b_dw"], p["ln_g"], p["ln_b"])
        state = u_full[:, -CONV_STATE:]
    return y, state


def _mlstm_layer(h, batch, state, p):
    m_rows = h.shape[0]
    t_len = m_rows // batch
    q, k, v, o, gates = _qkvo(h, p["w_q"], p["w_k"], p["w_v"], p["w_o"], p["w_if"], p["b_if"])
    gates = gates[:, :2 * N_HEADS]
    if state is None:
        hg, c_new, n_new, m_new = _cell_prompt(q, k, v, o, gates.T, p["g_mh"], batch)
    else:
        c0, n0, m0 = state
        pad_t = SAMPLE_PAD_T - t_len
        pad3 = lambda a: jnp.pad(a.reshape(batch, t_len, -1), ((0, 0), (0, pad_t), (0, 0)))
        g3 = gates.reshape(batch, t_len, 2 * N_HEADS)
        g_pad = jnp.concatenate(
            [jnp.full((batch, pad_t, N_HEADS), -jnp.inf, F32), jnp.zeros((batch, pad_t, N_HEADS), F32)], axis=2)
        gates_t = jnp.swapaxes(jnp.concatenate([g3, g_pad], axis=1), 1, 2)
        m0_lanes = jnp.broadcast_to(m0.astype(F32)[:, :, None], (batch, N_HEADS, LANES))
        hg, c_new, n_new, m_new = _cell_sample(
            pad3(q), pad3(k), pad3(v), o.reshape(batch, t_len, -1), gates_t, p["g_mh"],
            c0.astype(F32), n0.astype(F32), m0_lanes)
        hg = hg.reshape(m_rows, -1).astype(BF16)
    return hg, (c_new, n_new, m_new[:, :, 0])


def kernel(x_prompt, x_sample, cache_conv, state_C, state_n, state_m, conv_w_pw1, conv_b_pw1, conv_w_dw, conv_b_dw, conv_ln_g, conv_ln_b, conv_w_pw2, conv_b_pw2, mlstm_w_q, mlstm_w_k, mlstm_w_v, mlstm_w_o, mlstm_w_i, mlstm_b_i, mlstm_w_f, mlstm_b_f, mlstm_g_norm, mlstm_w_out, ffn_w_gate, ffn_w_up, ffn_w_down, norm_mix_pre, norm_mix_post, norm_ffn_pre, norm_ffn_post):
    depth = ffn_w_gate.shape[0]
    d = x_prompt.shape[-1]
    batches = (x_prompt.shape[0], x_sample.shape[0])
    xs = [x_prompt.reshape(-1, d), x_sample.reshape(-1, d)]
    hs = [None, None]
    conv_new = [[], []]
    c_new, n_new, m_new = [[], []], [[], []], [[], []]

    for i in range(depth):
        j = i // 2
        is_conv = i % 2 == 0
        if is_conv:
            p = {
                "g_pre": _row(norm_mix_pre[i]),
                "w_pw1": conv_w_pw1[j].astype(BF16), "b_pw1": _row(conv_b_pw1[j]),
                "w_dw": conv_w_dw[j].astype(F32), "b_dw": _row(conv_b_dw[j]),
                "ln_g": _row(conv_ln_g[j]), "ln_b": _row(conv_ln_b[j]),
            }
            w_proj, b_proj = conv_w_pw2[j].astype(BF16), _row(conv_b_pw2[j])
        else:
            w_if = jnp.concatenate([mlstm_w_i[j], mlstm_w_f[j]], axis=1)
            b_if = jnp.concatenate([mlstm_b_i[j], mlstm_b_f[j]], axis=0)
            pad = LANES - w_if.shape[1]
            p = {
                "w_q": mlstm_w_q[j].astype(BF16), "w_k": mlstm_w_k[j].astype(BF16),
                "w_v": mlstm_w_v[j].astype(BF16), "w_o": mlstm_w_o[j].astype(BF16),
                "w_if": jnp.pad(w_if, ((0, 0), (0, pad))).astype(BF16),
                "b_if": jnp.pad(_row(b_if), ((0, 0), (0, pad))),
                "g_mh": _row(mlstm_g_norm[j]),
            }
            w_proj, b_proj = mlstm_w_out[j].astype(BF16), None
        w_gate, w_up, w_down = (ffn_w_gate[i].astype(BF16), ffn_w_up[i].astype(BF16),
                                ffn_w_down[i].astype(BF16))
        next_is_mlstm = i + 1 < depth and (i + 1) % 2 == 1
        g_next = _row(norm_mix_pre[i + 1]) if next_is_mlstm else None

        for gi in range(2):
            x, batch = xs[gi], batches[gi]
            if is_conv:
                past = None if gi == 0 else cache_conv[j]
                mix_in, state = _conv_layer(x, batch, past, p)
                conv_new[gi].append(state.astype(cache_conv.dtype))
            else:
                state = None if gi == 0 else (state_C[j], state_n[j], state_m[j])
                mix_in, (c1, n1, m1) = _mlstm_layer(hs[gi], batch, state, p)
                c_new[gi].append(c1.astype(state_C.dtype))
                n_new[gi].append(n1.astype(state_n.dtype))
                m_new[gi].append(m1.astype(state_m.dtype))
            x, h_ffn = _proj_post(mix_in, w_proj, b_proj, x, _row(norm_mix_post[i]), _row(norm_ffn_pre[i]))
            xs[gi], hs[gi] = _ffn(h_ffn, w_gate, w_up, w_down, x, _row(norm_ffn_post[i]), g_next)

    return (xs[0].reshape(x_prompt.shape), xs[1].reshape(x_sample.shape),
            jnp.stack(conv_new[0]), jnp.stack(conv_new[1]),
            jnp.stack(c_new[0]), jnp.stack(n_new[0]), jnp.stack(m_new[0]),
            jnp.stack(c_new[1]), jnp.stack(n_new[1]), jnp.stack(m_new[1]))
```

```python
import functools

import jax
import jax.numpy as jnp
from jax import lax
from jax.experimental import pallas as pl
from jax.experimental.pallas import tpu as pltpu

F32 = jnp.float32
BF16 = jnp.bfloat16

D_MODEL = 2048
N_HEADS = 8
DV_HEAD = D_MODEL // N_HEADS
DQK_HEAD = DV_HEAD // 2
CONV_WIDTH = 31
CONV_STATE = CONV_WIDTH - 1
RMS_EPS = 1e-6
LN_EPS = 1e-5

LANES = 128
SUBLANES = 8
BF16_ROWS = 16
MIB = 1024 * 1024
VMEM_LIMIT = 52 * MIB

ROW_TILE = 1024
ROW_TILE_FULL_N = 512
CONV_TIME_TILE = 256
CONV_HALO = 32
CONV_ROW_STRIDE = 4
CELL_CHUNK = 256
SAMPLE_PAD_T = BF16_ROWS
SAMPLE_SEQS = 4
SAMPLE_UNROLL = 2
SAMPLE_CONV_SEQS = 16


def _compiler_params(semantics):
    return pltpu.CompilerParams(dimension_semantics=semantics, vmem_limit_bytes=VMEM_LIMIT)


def _dot(a, b):
    return jnp.dot(a, b, preferred_element_type=F32)


def _rms(x, g):
    return x * lax.rsqrt(jnp.mean(x * x, axis=-1, keepdims=True) + RMS_EPS) * g


def _sigmoid(x):
    return jax.nn.sigmoid(x)


def _pw1_glu_kernel(x_ref, g_ref, wa_ref, wg_ref, ba_ref, bg_ref, u_ref, h_scr):
    @pl.when(pl.program_id(1) == 0)
    def _():
        h_scr[...] = _rms(x_ref[...], g_ref[...]).astype(BF16)

    h = h_scr[...]
    a = _dot(h, wa_ref[...]) + ba_ref[...]
    g = _dot(h, wg_ref[...]) + bg_ref[...]
    u_ref[...] = a * _sigmoid(g)


def _pw1_glu(x, g_pre, w, b):
    m_rows, d = x.shape
    tm, tn = ROW_TILE, 512
    nn = d // tn
    return pl.pallas_call(
        _pw1_glu_kernel,
        grid=(m_rows // tm, nn),
        in_specs=[
            pl.BlockSpec((tm, d), lambda m, n: (m, 0)),
            pl.BlockSpec((1, d), lambda m, n: (0, 0)),
            pl.BlockSpec((d, tn), lambda m, n: (0, n)),
            pl.BlockSpec((d, tn), lambda m, n: (0, n + nn)),
            pl.BlockSpec((1, tn), lambda m, n: (0, n)),
            pl.BlockSpec((1, tn), lambda m, n: (0, n + nn)),
        ],
        out_specs=pl.BlockSpec((tm, tn), lambda m, n: (m, n)),
        out_shape=jax.ShapeDtypeStruct((m_rows, d), F32),
        scratch_shapes=[pltpu.VMEM((tm, d), BF16)],
        compiler_params=_compiler_params(("parallel", "arbitrary")),
        name="pw1_glu",
    )(x, g_pre, w, w, b, b)


def _ln_silu(v, ln_g, ln_b):
    xc = v - jnp.mean(v, axis=-1, keepdims=True)
    var = jnp.mean(xc * xc, axis=-1, keepdims=True)
    y = xc * lax.rsqrt(var + LN_EPS) * ln_g + ln_b
    return y * _sigmoid(y)


def _conv_prompt_kernel(u_ref, w_ref, b_ref, lg_ref, lb_ref, y_ref, ext, cv):
    tt, d = u_ref.shape
    nl = d // LANES
    first_tap = CONV_HALO - CONV_STATE
    block_rows = SUBLANES * CONV_ROW_STRIDE

    @pl.when(pl.program_id(1) == 0)
    def _():
        ext[:, 0:CONV_HALO, :] = jnp.zeros((nl, CONV_HALO, LANES), F32)

    @pl.when(pl.program_id(1) != 0)
    def _():
        ext[:, 0:CONV_HALO, :] = ext[:, tt:tt + CONV_HALO, :]

    for lt in range(nl):
        ext[lt, CONV_HALO:CONV_HALO + tt, :] = u_ref[:, lt * LANES:(lt + 1) * LANES]

    def lane_tile(lt, carry):
        lanes = pl.ds(pl.multiple_of(lt * LANES, LANES), LANES)
        bias = jnp.broadcast_to(b_ref[:, lanes], (SUBLANES, LANES))
        for r0 in range(0, tt, block_rows):
            accs = [bias] * CONV_ROW_STRIDE
            window = {}
            for j in range(CONV_WIDTH):
                wj = jnp.broadcast_to(w_ref[j:j + 1, lanes], (SUBLANES, LANES))
                for s in range(CONV_ROW_STRIDE):
                    if s + j not in window:
                        window[s + j] = ext[lt, pl.ds(r0 + first_tap + s + j, SUBLANES, stride=CONV_ROW_STRIDE), :]
                    accs[s] = accs[s] + wj * window[s + j]
            for s in range(CONV_ROW_STRIDE):
                cv[lt, pl.ds(r0 + s, SUBLANES, stride=CONV_ROW_STRIDE), :] = accs[s]
        return carry

    lax.fori_loop(0, nl, lane_tile, 0)

    v = cv[...]
    xc = v - jnp.sum(jnp.sum(v, axis=0), axis=-1, keepdims=True) / d
    var = jnp.sum(jnp.sum(xc * xc, axis=0), axis=-1, keepdims=True) / d
    inv = lax.rsqrt(var + LN_EPS)
    for lt in range(nl):
        cols = slice(lt * LANES, (lt + 1) * LANES)
        y = xc[lt] * inv * lg_ref[:, cols] + lb_ref[:, cols]
        y_ref[:, cols] = (y * _sigmoid(y)).astype(BF16)


def _conv_prompt(u, batch, w_dw, b_dw, ln_g, ln_b):
    m_rows, d = u.shape
    tt = CONV_TIME_TILE
    nt = m_rows // batch // tt
    vec = pl.BlockSpec((1, d), lambda b, t: (0, 0))
    return pl.pallas_call(
        _conv_prompt_kernel,
        grid=(batch, nt),
        in_specs=[
            pl.BlockSpec((tt, d), lambda b, t: (b * nt + t, 0)),
            pl.BlockSpec((CONV_WIDTH, d), lambda b, t: (0, 0)),
            vec, vec, vec,
        ],
        out_specs=pl.BlockSpec((tt, d), lambda b, t: (b * nt + t, 0)),
        out_shape=jax.ShapeDtypeStruct((m_rows, d), BF16),
        scratch_shapes=[pltpu.VMEM((d // LANES, CONV_HALO + tt, LANES), F32),
                        pltpu.VMEM((d // LANES, tt, LANES), F32)],
        compiler_params=_compiler_params(("parallel", "arbitrary")),
        name="conv_prompt",
    )(u, w_dw, b_dw, ln_g, ln_b)


def _conv_sample_kernel(cache_ref, u_ref, w_ref, b_ref, lg_ref, lb_ref, y_ref, state_ref, cv):
    hist, nseq, d = cache_ref.shape
    t_new = u_ref.shape[0]
    state_ref[0:hist - t_new] = cache_ref[t_new:hist]
    state_ref[hist - t_new:hist] = u_ref[...]

    def history_row(r, seqs, lanes):
        if r < hist:
            return cache_ref[r, seqs, lanes]
        return u_ref[r - hist, seqs, lanes]

    def lane_tile(lt, carry):
        lanes = pl.ds(pl.multiple_of(lt * LANES, LANES), LANES)
        bias = jnp.broadcast_to(b_ref[:, lanes], (SUBLANES, LANES))
        for s0 in range(0, nseq, SUBLANES):
            seqs = slice(s0, s0 + SUBLANES)
            accs = [bias] * t_new
            for j in range(CONV_WIDTH):
                wj = jnp.broadcast_to(w_ref[j:j + 1, lanes], (SUBLANES, LANES))
                for t in range(t_new):
                    accs[t] = accs[t] + wj * history_row(t + j, seqs, lanes)
            for t in range(t_new):
                cv[t, seqs, lanes] = accs[t]
        return carry

    lax.fori_loop(0, d // LANES, lane_tile, 0)
    y_ref[...] = _ln_silu(cv[...], lg_ref[...], lb_ref[...]).astype(BF16)


def _conv_sample(cache_t, u_t, w_dw, b_dw, ln_g, ln_b):
    hist, nb, d = cache_t.shape
    t_new = u_t.shape[0]
    nseq = SAMPLE_CONV_SEQS
    vec = pl.BlockSpec((1, d), lambda i: (0, 0))
    seq_block = lambda rows: pl.BlockSpec((rows, nseq, d), lambda i: (0, i, 0))
    return pl.pallas_call(
        _conv_sample_kernel,
        grid=(nb // nseq,),
        in_specs=[
            seq_block(hist), seq_block(t_new),
            pl.BlockSpec((CONV_WIDTH, d), lambda i: (0, 0)),
            vec, vec, vec,
        ],
        out_specs=[seq_block(t_new), seq_block(hist)],
        out_shape=[jax.ShapeDtypeStruct((t_new, nb, d), BF16),
                   jax.ShapeDtypeStruct((hist, nb, d), F32)],
        scratch_shapes=[pltpu.VMEM((t_new, nseq, d), F32)],
        compiler_params=_compiler_params(("parallel",)),
        name="conv_sample",
    )(cache_t, u_t, w_dw, b_dw, ln_g, ln_b)


def _proj_post_kernel(has_bias, *refs):
    if has_bias:
        a_ref, w_ref, b_ref, x_ref, gpost_ref, gnext_ref, xo_ref, ho_ref = refs
    else:
        a_ref, w_ref, x_ref, gpost_ref, gnext_ref, xo_ref, ho_ref = refs
    mix = _dot(a_ref[...], w_ref[...])
    if has_bias:
        mix = mix + b_ref[...]
    x1 = x_ref[...] + _rms(mix, gpost_ref[...])
    xo_ref[...] = x1
    ho_ref[...] = _rms(x1, gnext_ref[...]).astype(BF16)


def _proj_post(a, w, b, x, g_post, g_next):
    m_rows, k = a.shape
    d = w.shape[1]
    tm = ROW_TILE_FULL_N
    row = lambda width: pl.BlockSpec((tm, width), lambda m: (m, 0))
    vec = pl.BlockSpec((1, d), lambda m: (0, 0))
    has_bias = b is not None
    in_specs = [row(k), pl.BlockSpec((k, d), lambda m: (0, 0))]
    args = [a, w]
    if has_bias:
        in_specs.append(vec)
        args.append(b)
    in_specs += [row(d), vec, vec]
    args += [x, g_post, g_next]
    return pl.pallas_call(
        functools.partial(_proj_post_kernel, has_bias),
        grid=(m_rows // tm,),
        in_specs=in_specs,
        out_specs=[row(d), row(d)],
        out_shape=[jax.ShapeDtypeStruct((m_rows, d), F32), jax.ShapeDtypeStruct((m_rows, d), BF16)],
        compiler_params=_compiler_params(("parallel",)),
        name="proj_post",
    )(*args)


def _ffn_kernel(has_next, *refs):
    if has_next:
        h_ref, wg_ref, wu_ref, wd_ref, x_ref, gpost_ref, gnext_ref, xo_ref, ho_ref, acc = refs
    else:
        h_ref, wg_ref, wu_ref, wd_ref, x_ref, gpost_ref, xo_ref, acc = refs
    f = pl.program_id(1)

    @pl.when(f == 0)
    def _():
        acc[...] = jnp.zeros(acc.shape, F32)

    h = h_ref[...]
    gate = _dot(h, wg_ref[...])
    up = _dot(h, wu_ref[...])
    acc[...] += _dot((gate * _sigmoid(gate) * up).astype(BF16), wd_ref[...])

    @pl.when(f == pl.num_programs(1) - 1)
    def _():
        x2 = x_ref[...] + _rms(acc[...], gpost_ref[...])
        xo_ref[...] = x2
        if has_next:
            ho_ref[...] = _rms(x2, gnext_ref[...]).astype(BF16)


def _ffn(h, w_gate, w_up, w_down, layer, x, g_post, g_next):
    m_rows, d = h.shape
    d_ff = w_gate.shape[2]
    tm, tf = ROW_TILE_FULL_N, 512
    row = pl.BlockSpec((tm, d), lambda m, f: (m, 0))
    vec = pl.BlockSpec((1, d), lambda m, f: (0, 0))
    has_next = g_next is not None
    in_specs = [
        row,
        pl.BlockSpec((None, d, tf), lambda m, f: (layer, 0, f)),
        pl.BlockSpec((None, d, tf), lambda m, f: (layer, 0, f)),
        pl.BlockSpec((None, tf, d), lambda m, f: (layer, f, 0)),
        row, vec,
    ]
    args = [h, w_gate, w_up, w_down, x, g_post]
    out_specs = [row]
    out_shape = [jax.ShapeDtypeStruct((m_rows, d), F32)]
    if has_next:
        in_specs.append(vec)
        args.append(g_next)
        out_specs.append(row)
        out_shape.append(jax.ShapeDtypeStruct((m_rows, d), BF16))
    outs = pl.pallas_call(
        functools.partial(_ffn_kernel, has_next),
        grid=(m_rows // tm, d_ff // tf),
        in_specs=in_specs,
        out_specs=out_specs,
        out_shape=out_shape,
        scratch_shapes=[pltpu.VMEM((tm, d), F32)],
        compiler_params=_compiler_params(("parallel", "arbitrary")),
        name="ffn",
    )(*args)
    return (outs[0], outs[1]) if has_next else (outs[0], None)


def _qkvo_kernel(h_ref, wq_ref, wk_ref, wv_ref, wo_ref, wif_ref, bif_ref,
                 q_ref, k_ref, v_ref, o_ref, g_ref):
    h = h_ref[...]
    q_ref[...] = (_dot(h, wq_ref[...]) * (DQK_HEAD ** -0.5)).astype(BF16)
    k_ref[...] = _dot(h, wk_ref[...]).astype(BF16)
    v_ref[...] = _dot(h, wv_ref[...]).astype(BF16)
    o_ref[...] = _sigmoid(_dot(h, wo_ref[...]))

    @pl.when(pl.program_id(1) == 0)
    def _():
        z = _dot(h, wif_ref[...]) + bif_ref[...]
        lane = lax.broadcasted_iota(jnp.int32, z.shape, 1)
        log_sig = jnp.minimum(z, 0.0) - jnp.log1p(jnp.exp(-jnp.abs(z)))
        g_ref[...] = jnp.where(lane < N_HEADS, z, log_sig)


def _qkvo(h, w_q, w_k, w_v, w_o, w_if, b_if):
    m_rows, d = h.shape
    hk, hv = w_q.shape[1], w_v.shape[1]
    tm, steps = ROW_TILE, 4
    col = lambda width: pl.BlockSpec((d, width), lambda m, n: (0, n))
    out = lambda width: pl.BlockSpec((tm, width), lambda m, n: (m, n))
    return pl.pallas_call(
        _qkvo_kernel,
        grid=(m_rows // tm, steps),
        in_specs=[
            pl.BlockSpec((tm, d), lambda m, n: (m, 0)),
            col(hk // steps), col(hk // steps), col(hv // steps), col(hv // steps),
            pl.BlockSpec((d, LANES), lambda m, n: (0, 0)),
            pl.BlockSpec((1, LANES), lambda m, n: (0, 0)),
        ],
        out_specs=[out(hk // steps), out(hk // steps), out(hv // steps), out(hv // steps),
                   pl.BlockSpec((tm, LANES), lambda m, n: (m, 0))],
        out_shape=[
            jax.ShapeDtypeStruct((m_rows, hk), BF16),
            jax.ShapeDtypeStruct((m_rows, hk), BF16),
            jax.ShapeDtypeStruct((m_rows, hv), BF16),
            jax.ShapeDtypeStruct((m_rows, hv), F32),
            jax.ShapeDtypeStruct((m_rows, LANES), F32),
        ],
        compiler_params=_compiler_params(("parallel", "arbitrary")),
        name="qkvo",
    )(h, w_q, w_k, w_v, w_o, w_if, b_if)


def _cell_chunks(q, k, v, log_i, log_f, c_state, n_state, m_state):
    probs = range(len(q))
    length = q[0].shape[0]
    t_idx = lax.broadcasted_iota(jnp.int32, (length, length), 0)
    s_idx = lax.broadcasted_iota(jnp.int32, (length, length), 1)
    causal = s_idx <= t_idx
    eye = s_idx == t_idx
    nt_dims = (((1,), (1,)), ((), ()))
    tn_dims = (((0,), (0,)), ((), ()))

    qk = [lax.dot_general(q[p], k[p], nt_dims, preferred_element_type=F32) for p in probs]
    inter = [_dot(q[p], c_state[p].astype(BF16)) for p in probs]
    b_col = [jnp.sum(jnp.where(causal, log_f[p], 0.0), axis=1, keepdims=True) for p in probs]
    b_row = [jnp.sum(jnp.where(eye, b_col[p], 0.0), axis=0, keepdims=True) for p in probs]
    i_col = [jnp.sum(jnp.where(eye, log_i[p], 0.0), axis=1, keepdims=True) for p in probs]
    a_inter = [b_col[p] + m_state[p] for p in probs]
    d_intra = [jnp.where(causal, b_col[p] - b_row[p] + log_i[p], -jnp.inf) for p in probs]
    m_t = [jnp.maximum(a_inter[p], jnp.max(d_intra[p], axis=1, keepdims=True)) for p in probs]
    w_inter = [jnp.exp(a_inter[p] - m_t[p]) for p in probs]
    s = [qk[p] * jnp.exp(d_intra[p] - m_t[p]) for p in probs]
    intra = [_dot(s[p].astype(BF16), v[p]) for p in probs]

    def den(p):
        qn = w_inter[p] * (q[p].astype(F32) * n_state[p])
        if length % LANES == 0:
            for c0 in range(0, length, LANES):
                qn = qn + s[p][:, c0:c0 + LANES]
            return jnp.sum(qn, axis=1, keepdims=True)
        return jnp.sum(qn, axis=1, keepdims=True) + jnp.sum(s[p], axis=1, keepdims=True)

    h = [(w_inter[p] * inter[p] + intra[p]) / jnp.maximum(jnp.abs(den(p)), jnp.exp(-m_t[p]))
         for p in probs]

    m_new = [m_t[p][length - 1:length, :] for p in probs]
    b_last = [b_col[p][length - 1:length, :] for p in probs]
    g_state = [jnp.exp(b_last[p] + m_state[p] - m_new[p]) for p in probs]
    kw = [k[p].astype(F32) * jnp.exp(b_last[p] - b_col[p] + i_col[p] - m_new[p]) for p in probs]
    c_new = [g_state[p] * c_state[p]
             + lax.dot_general(kw[p].astype(BF16), v[p], tn_dims, preferred_element_type=F32)
             for p in probs]
    n_new = [g_state[p] * n_state[p] + jnp.sum(kw[p], axis=0, keepdims=True) for p in probs]
    return h, c_new, n_new, m_new


def _head_norm_gate(h, o, g_mh):
    hn = h * lax.rsqrt(jnp.mean(h * h, axis=-1, keepdims=True) + RMS_EPS)
    return o * (hn * g_mh)


def _cell_prompt_kernel(q_ref, k_ref, v_ref, o_ref, gt_ref, gmh_ref,
                        hg_ref, c_out, n_out, m_out, c_scr, n_scr, m_scr):
    c = pl.program_id(1)

    @pl.when(c == 0)
    def _():
        c_scr[...] = jnp.zeros(c_scr.shape, F32)
        n_scr[...] = jnp.zeros(n_scr.shape, F32)
        m_scr[...] = jnp.zeros(m_scr.shape, F32)

    heads = range(N_HEADS)
    qk_cols = [slice(hd * DQK_HEAD, (hd + 1) * DQK_HEAD) for hd in heads]
    v_cols = [slice(hd * DV_HEAD, (hd + 1) * DV_HEAD) for hd in heads]
    h, c_new, n_new, m_new = _cell_chunks(
        [q_ref[:, qk_cols[hd]] for hd in heads], [k_ref[:, qk_cols[hd]] for hd in heads],
        [v_ref[:, v_cols[hd]] for hd in heads],
        [gt_ref[hd:hd + 1, :] for hd in heads],
        [gt_ref[N_HEADS + hd:N_HEADS + hd + 1, :] for hd in heads],
        [c_scr[hd] for hd in heads], [n_scr[hd:hd + 1, :] for hd in heads],
        [m_scr[hd:hd + 1, 0:1] for hd in heads])
    for hd in heads:
        c_scr[hd] = c_new[hd]
        n_scr[hd:hd + 1, :] = n_new[hd]
        m_scr[hd:hd + 1, :] = jnp.broadcast_to(m_new[hd], (1, LANES))
        hg_ref[:, v_cols[hd]] = _head_norm_gate(
            h[hd], o_ref[:, v_cols[hd]], gmh_ref[:, v_cols[hd]]).astype(BF16)

    @pl.when(c == pl.num_programs(1) - 1)
    def _():
        c_out[0] = c_scr[...]
        n_out[0] = n_scr[...]
        m_out[0] = m_scr[...]


def _cell_prompt(q, k, v, o, gates_t, g_mh, batch):
    m_rows, hk = q.shape
    hv = v.shape[1]
    length = CELL_CHUNK
    nc = m_rows // batch // length
    rows = lambda width: pl.BlockSpec((length, width), lambda b, c: (b * nc + c, 0))
    return pl.pallas_call(
        _cell_prompt_kernel,
        grid=(batch, nc),
        in_specs=[
            rows(hk), rows(hk), rows(hv), rows(hv),
            pl.BlockSpec((2 * N_HEADS, length), lambda b, c: (0, b * nc + c)),
            pl.BlockSpec((1, hv), lambda b, c: (0, 0)),
        ],
        out_specs=[
            rows(hv),
            pl.BlockSpec((1, N_HEADS, DQK_HEAD, DV_HEAD), lambda b, c: (b, 0, 0, 0)),
            pl.BlockSpec((1, N_HEADS, DQK_HEAD), lambda b, c: (b, 0, 0)),
            pl.BlockSpec((1, N_HEADS, LANES), lambda b, c: (b, 0, 0)),
        ],
        out_shape=[
            jax.ShapeDtypeStruct((m_rows, hv), BF16),
            jax.ShapeDtypeStruct((batch, N_HEADS, DQK_HEAD, DV_HEAD), F32),
            jax.ShapeDtypeStruct((batch, N_HEADS, DQK_HEAD), F32),
            jax.ShapeDtypeStruct((batch, N_HEADS, LANES), F32),
        ],
        scratch_shapes=[
            pltpu.VMEM((N_HEADS, DQK_HEAD, DV_HEAD), F32),
            pltpu.VMEM((N_HEADS, DQK_HEAD), F32),
            pltpu.VMEM((N_HEADS, LANES), F32),
        ],
        compiler_params=_compiler_params(("parallel", "arbitrary")),
        name="cell_prompt",
    )(q, k, v, o, gates_t, g_mh)


def _cell_sample_kernel(q_ref, k_ref, v_ref, o_ref, gt_ref, gmh_ref, c_in, n_in, m_in,
                        hg_ref, c_out, n_out, m_out):
    t_new = o_ref.shape[1]

    heads = range(N_HEADS)
    qk_cols = [slice(hd * DQK_HEAD, (hd + 1) * DQK_HEAD) for hd in heads]
    v_cols = [slice(hd * DV_HEAD, (hd + 1) * DV_HEAD) for hd in heads]

    def sequence_group(g, carry):
        probs = [(g * SAMPLE_UNROLL + u, hd) for u in range(SAMPLE_UNROLL) for hd in heads]
        h, c_new, n_new, m_new = _cell_chunks(
            [q_ref[b, :, qk_cols[hd]] for b, hd in probs], [k_ref[b, :, qk_cols[hd]] for b, hd in probs],
            [v_ref[b, :, v_cols[hd]] for b, hd in probs],
            [gt_ref[b, hd:hd + 1, :] for b, hd in probs],
            [gt_ref[b, N_HEADS + hd:N_HEADS + hd + 1, :] for b, hd in probs],
            [c_in[b, hd] for b, hd in probs], [n_in[b, hd:hd + 1, :] for b, hd in probs],
            [m_in[b, hd:hd + 1, 0:1] for b, hd in probs])
        for p, (b, hd) in enumerate(probs):
            c_out[b, hd] = c_new[p]
            n_out[b, hd:hd + 1, :] = n_new[p]
            m_out[b, hd:hd + 1, :] = jnp.broadcast_to(m_new[p], (1, LANES))
            hg_ref[b, :, v_cols[hd]] = _head_norm_gate(
                h[p][0:t_new, :], o_ref[b, :, v_cols[hd]], gmh_ref[:, v_cols[hd]])
        return carry

    lax.fori_loop(0, q_ref.shape[0] // SAMPLE_UNROLL, sequence_group, 0)


def _cell_sample(q, k, v, o, gates_t, g_mh, c0, n0, m0):
    nb, tp, hk = q.shape
    hv = v.shape[2]
    t_new = o.shape[1]
    nseq = SAMPLE_SEQS
    blk3 = lambda s1, s2: pl.BlockSpec((nseq, s1, s2), lambda i: (i, 0, 0))
    c_spec = pl.BlockSpec((nseq, N_HEADS, DQK_HEAD, DV_HEAD), lambda i: (i, 0, 0, 0))
    return pl.pallas_call(
        _cell_sample_kernel,
        grid=(nb // nseq,),
        in_specs=[
            blk3(tp, hk), blk3(tp, hk), blk3(tp, hv), blk3(t_new, hv),
            blk3(2 * N_HEADS, tp),
            pl.BlockSpec((1, hv), lambda i: (0, 0)),
            c_spec, blk3(N_HEADS, DQK_HEAD), blk3(N_HEADS, LANES),
        ],
        out_specs=[blk3(t_new, hv), c_spec, blk3(N_HEADS, DQK_HEAD), blk3(N_HEADS, LANES)],
        out_shape=[
            jax.ShapeDtypeStruct((nb, t_new, hv), F32),
            jax.ShapeDtypeStruct((nb, N_HEADS, DQK_HEAD, DV_HEAD), F32),
            jax.ShapeDtypeStruct((nb, N_HEADS, DQK_HEAD), F32),
            jax.ShapeDtypeStruct((nb, N_HEADS, LANES), F32),
        ],
        compiler_params=_compiler_params(("parallel",)),
        name="cell_sample",
    )(q, k, v, o, gates_t, g_mh, c0, n0, m0)


def _row(v):
    return v.reshape(1, -1).astype(F32)


def _swap_row_order(a, outer, inner):
    return jnp.swapaxes(a.reshape(outer, inner, -1), 0, 1).reshape(outer * inner, -1)


def _conv_layer(x, batch, past, p):
    m_rows, d = x.shape
    u = _pw1_glu(x, p["g_pre"], p["w_pw1"], p["b_pw1"])
    if past is None:
        y = _conv_prompt(u, batch, p["w_dw"], p["b_dw"], p["ln_g"], p["ln_b"])
        state = u.reshape(batch, -1, d)[:, -CONV_STATE:]
    else:
        y_t, state_t = _conv_sample(
            jnp.swapaxes(past.astype(F32), 0, 1), u.reshape(m_rows // batch, batch, d),
            p["w_dw"], p["b_dw"], p["ln_g"], p["ln_b"])
        y = y_t.reshape(m_rows, d)
        state = jnp.swapaxes(state_t, 0, 1)
    return y, state


def _mlstm_layer(h, batch, state, p):
    m_rows = h.shape[0]
    t_len = m_rows // batch
    q, k, v, o, gates = _qkvo(h, p["w_q"], p["w_k"], p["w_v"], p["w_o"], p["w_if"], p["b_if"])
    gates = gates[:, :2 * N_HEADS]
    if state is None:
        hg, c_new, n_new, m_new = _cell_prompt(q, k, v, o, gates.T, p["g_mh"], batch)
    else:
        c0, n0, m0 = state
        pad_t = SAMPLE_PAD_T - t_len
        pad3 = lambda a: jnp.pad(a.reshape(batch, t_len, -1), ((0, 0), (0, pad_t), (0, 0)))
        g3 = gates.reshape(batch, t_len, 2 * N_HEADS)
        g_pad = jnp.concatenate(
            [jnp.full((batch, pad_t, N_HEADS), -jnp.inf, F32), jnp.zeros((batch, pad_t, N_HEADS), F32)], axis=2)
        gates_t = jnp.swapaxes(jnp.concatenate([g3, g_pad], axis=1), 1, 2)
        m0_lanes = jnp.broadcast_to(m0.astype(F32)[:, :, None], (batch, N_HEADS, LANES))
        hg, c_new, n_new, m_new = _cell_sample(
            pad3(q), pad3(k), pad3(v), o.reshape(batch, t_len, -1), gates_t, p["g_mh"],
            c0.astype(F32), n0.astype(F32), m0_lanes)
        hg = hg.reshape(m_rows, -1).astype(BF16)
    return hg, (c_new, n_new, m_new[:, :, 0])


def kernel(x_prompt, x_sample, cache_conv, state_C, state_n, state_m, conv_w_pw1, conv_b_pw1, conv_w_dw, conv_b_dw, conv_ln_g, conv_ln_b, conv_w_pw2, conv_b_pw2, mlstm_w_q, mlstm_w_k, mlstm_w_v, mlstm_w_o, mlstm_w_i, mlstm_b_i, mlstm_w_f, mlstm_b_f, mlstm_g_norm, mlstm_w_out, ffn_w_gate, ffn_w_up, ffn_w_down, norm_mix_pre, norm_mix_post, norm_ffn_pre, norm_ffn_post):
    depth = ffn_w_gate.shape[0]
    d = x_prompt.shape[-1]
    batches = (x_prompt.shape[0], x_sample.shape[0])
    xs = [x_prompt.reshape(-1, d), x_sample.reshape(-1, d)]
    hs = [None, None]
    conv_new = [[], []]
    c_new, n_new, m_new = [[], []], [[], []], [[], []]
    w_gate, w_up, w_down = ffn_w_gate.astype(BF16), ffn_w_up.astype(BF16), ffn_w_down.astype(BF16)
    sample_batch, sample_t = x_sample.shape[0], x_sample.shape[1]
    sample_time_major = False

    for i in range(depth):
        j = i // 2
        is_conv = i % 2 == 0
        if is_conv:
            p = {
                "g_pre": _row(norm_mix_pre[i]),
                "w_pw1": conv_w_pw1[j].astype(BF16), "b_pw1": _row(conv_b_pw1[j]),
                "w_dw": conv_w_dw[j].astype(F32), "b_dw": _row(conv_b_dw[j]),
                "ln_g": _row(conv_ln_g[j]), "ln_b": _row(conv_ln_b[j]),
            }
            w_proj, b_proj = conv_w_pw2[j].astype(BF16), _row(conv_b_pw2[j])
        else:
            w_if = jnp.concatenate([mlstm_w_i[j], mlstm_w_f[j]], axis=1)
            b_if = jnp.concatenate([mlstm_b_i[j], mlstm_b_f[j]], axis=0)
            pad = LANES - w_if.shape[1]
            p = {
                "w_q": mlstm_w_q[j].astype(BF16), "w_k": mlstm_w_k[j].astype(BF16),
                "w_v": mlstm_w_v[j].astype(BF16), "w_o": mlstm_w_o[j].astype(BF16),
                "w_if": jnp.pad(w_if, ((0, 0), (0, pad))).astype(BF16),
                "b_if": jnp.pad(_row(b_if), ((0, 0), (0, pad))),
                "g_mh": _row(mlstm_g_norm[j]),
            }
            w_proj, b_proj = mlstm_w_out[j].astype(BF16), None
        next_is_mlstm = i + 1 < depth and (i + 1) % 2 == 1
        g_next = _row(norm_mix_pre[i + 1]) if next_is_mlstm else None

        if sample_time_major != is_conv:
            dims = (sample_t, sample_batch) if sample_time_major else (sample_batch, sample_t)
            xs[1] = _swap_row_order(xs[1], *dims)
            hs[1] = None if hs[1] is None else _swap_row_order(hs[1], *dims)
            sample_time_major = is_conv

        for gi in range(2):
            x, batch = xs[gi], batches[gi]
            if is_conv:
                past = None if gi == 0 else cache_conv[j]
                mix_in, state = _conv_layer(x, batch, past, p)
                conv_new[gi].append(state.astype(cache_conv.dtype))
            else:
                state = None if gi == 0 else (state_C[j], state_n[j], state_m[j])
                mix_in, (c1, n1, m1) = _mlstm_layer(hs[gi], batch, state, p)
                c_new[gi].append(c1.astype(state_C.dtype))
                n_new[gi].append(n1.astype(state_n.dtype))
                m_new[gi].append(m1.astype(state_m.dtype))
            x, h_ffn = _proj_post(mix_in, w_proj, b_proj, x, _row(norm_mix_post[i]), _row(norm_ffn_pre[i]))
            xs[gi], hs[gi] = _ffn(h_ffn, w_gate, w_up, w_down, i, x, _row(norm_ffn_post[i]), g_next)

    if sample_time_major:
        xs[1] = _swap_row_order(xs[1], sample_t, sample_batch)

    return (xs[0].reshape(x_prompt.shape), xs[1].reshape(x_sample.shape),
            jnp.stack(conv_new[0]), jnp.stack(conv_new[1]),
            jnp.stack(c_new[0]), jnp.stack(n_new[0]), jnp.stack(m_new[0]),
            jnp.stack(c_new[1]), jnp.stack(n_new[1]), jnp.stack(m_new[1]))
```

```python
import functools

import jax
import jax.numpy as jnp
from jax import lax
from jax.experimental import pallas as pl
from jax.experimental.pallas import tpu as pltpu

F32 = jnp.float32
BF16 = jnp.bfloat16

D_MODEL = 2048
N_HEADS = 8
DV_HEAD = D_MODEL // N_HEADS
DQK_HEAD = DV_HEAD // 2
CONV_WIDTH = 31
CONV_STATE = CONV_WIDTH - 1
RMS_EPS = 1e-6
LN_EPS = 1e-5

LANES = 128
SUBLANES = 8
BF16_ROWS = 16
MIB = 1024 * 1024
VMEM_LIMIT = 52 * MIB

ROW_TILE = 1024
ROW_TILE_FULL_N = 512
CONV_TIME_TILE = 256
CONV_HALO = 32
CONV_ROW_STRIDE = 4
CELL_CHUNK = 256
SAMPLE_PAD_T = BF16_ROWS
SAMPLE_SEQS = 4
SAMPLE_UNROLL = 2
SAMPLE_CONV_SEQS = 16


def _compiler_params(semantics):
    return pltpu.CompilerParams(dimension_semantics=semantics, vmem_limit_bytes=VMEM_LIMIT)


def _dot(a, b):
    return jnp.dot(a, b, preferred_element_type=F32)


def _rms(x, g):
    return x * lax.rsqrt(jnp.mean(x * x, axis=-1, keepdims=True) + RMS_EPS) * g


def _sigmoid(x):
    return jax.nn.sigmoid(x)


class _SideCasts:
    def __init__(self, weights, steps, step_of):
        self.args, self.in_specs, self.out_specs, self.out_shapes = [], [], [], []
        for w, idx in weights:
            _, r, c = w.shape
            share = 1
            while (r * share) % steps or (r * share // steps) % BF16_ROWS:
                share *= 2
            rows = r * share // steps
            self.args.append(w)
            self.in_specs.append(pl.BlockSpec(
                (None, rows, c), lambda *g, idx=idx, share=share: (idx, step_of(*g) // share, 0)))
            self.out_specs.append(pl.BlockSpec(
                (rows, c), lambda *g, share=share: (step_of(*g) // share, 0)))
            self.out_shapes.append(jax.ShapeDtypeStruct((r, c), BF16))

    def __len__(self):
        return len(self.args)


def _run_side_casts(src_refs, dst_refs):
    for src, dst in zip(src_refs, dst_refs):
        dst[...] = src[...].astype(BF16)


def _pw1_glu_kernel(n_side, *refs):
    x_ref, g_ref, wa_ref, wg_ref, ba_ref, bg_ref = refs[:6]
    side_in = refs[6:6 + n_side]
    u_ref = refs[6 + n_side]
    side_out = refs[7 + n_side:7 + 2 * n_side]
    h_scr = refs[7 + 2 * n_side]

    @pl.when(pl.program_id(1) == 0)
    def _():
        h_scr[...] = _rms(x_ref[...], g_ref[...]).astype(BF16)

    h = h_scr[...]
    a = _dot(h, wa_ref[...]) + ba_ref[...]
    g = _dot(h, wg_ref[...]) + bg_ref[...]
    u_ref[...] = a * _sigmoid(g)
    _run_side_casts(side_in, side_out)


def _pw1_glu(x, g_pre, w, b, side_weights=()):
    m_rows, d = x.shape
    tm, tn = ROW_TILE, 512
    nn = d // tn
    side = _SideCasts(side_weights, (m_rows // tm) * nn, lambda m, n: m * nn + n)
    outs = pl.pallas_call(
        functools.partial(_pw1_glu_kernel, len(side)),
        grid=(m_rows // tm, nn),
        in_specs=[
            pl.BlockSpec((tm, d), lambda m, n: (m, 0)),
            pl.BlockSpec((1, d), lambda m, n: (0, 0)),
            pl.BlockSpec((d, tn), lambda m, n: (0, n)),
            pl.BlockSpec((d, tn), lambda m, n: (0, n + nn)),
            pl.BlockSpec((1, tn), lambda m, n: (0, n)),
            pl.BlockSpec((1, tn), lambda m, n: (0, n + nn)),
        ] + side.in_specs,
        out_specs=[pl.BlockSpec((tm, tn), lambda m, n: (m, n))] + side.out_specs,
        out_shape=[jax.ShapeDtypeStruct((m_rows, d), F32)] + side.out_shapes,
        scratch_shapes=[pltpu.VMEM((tm, d), BF16)],
        compiler_params=_compiler_params(("parallel", "arbitrary")),
        name="pw1_glu",
    )(x, g_pre, w, w, b, b, *side.args)
    return outs[0], list(outs[1:])


def _ln_silu(v, ln_g, ln_b):
    xc = v - jnp.mean(v, axis=-1, keepdims=True)
    var = jnp.mean(xc * xc, axis=-1, keepdims=True)
    y = xc * lax.rsqrt(var + LN_EPS) * ln_g + ln_b
    return y * _sigmoid(y)


def _conv_prompt_kernel(u_ref, w_ref, b_ref, lg_ref, lb_ref, y_ref, ext, cv):
    tt, d = u_ref.shape
    nl = d // LANES
    first_tap = CONV_HALO - CONV_STATE
    block_rows = SUBLANES * CONV_ROW_STRIDE

    @pl.when(pl.program_id(1) == 0)
    def _():
        ext[:, 0:CONV_HALO, :] = jnp.zeros((nl, CONV_HALO, LANES), F32)

    @pl.when(pl.program_id(1) != 0)
    def _():
        ext[:, 0:CONV_HALO, :] = ext[:, tt:tt + CONV_HALO, :]

    for lt in range(nl):
        ext[lt, CONV_HALO:CONV_HALO + tt, :] = u_ref[:, lt * LANES:(lt + 1) * LANES]

    def lane_tile(lt, carry):
        lanes = pl.ds(pl.multiple_of(lt * LANES, LANES), LANES)
        bias = jnp.broadcast_to(b_ref[:, lanes], (SUBLANES, LANES))
        for r0 in range(0, tt, block_rows):
            accs = [bias] * CONV_ROW_STRIDE
            window = {}
            for j in range(CONV_WIDTH):
                wj = jnp.broadcast_to(w_ref[j:j + 1, lanes], (SUBLANES, LANES))
                for s in range(CONV_ROW_STRIDE):
                    if s + j not in window:
                        window[s + j] = ext[lt, pl.ds(r0 + first_tap + s + j, SUBLANES, stride=CONV_ROW_STRIDE), :]
                    accs[s] = accs[s] + wj * window[s + j]
            for s in range(CONV_ROW_STRIDE):
                cv[lt, pl.ds(r0 + s, SUBLANES, stride=CONV_ROW_STRIDE), :] = accs[s]
        return carry

    lax.fori_loop(0, nl, lane_tile, 0)

    v = cv[...]
    xc = v - jnp.sum(jnp.sum(v, axis=0), axis=-1, keepdims=True) / d
    var = jnp.sum(jnp.sum(xc * xc, axis=0), axis=-1, keepdims=True) / d
    inv = lax.rsqrt(var + LN_EPS)
    for lt in range(nl):
        cols = slice(lt * LANES, (lt + 1) * LANES)
        y = xc[lt] * inv * lg_ref[:, cols] + lb_ref[:, cols]
        y_ref[:, cols] = (y * _sigmoid(y)).astype(BF16)


def _conv_prompt(u, batch, w_dw, b_dw, ln_g, ln_b):
    m_rows, d = u.shape
    tt = CONV_TIME_TILE
    nt = m_rows // batch // tt
    vec = pl.BlockSpec((1, d), lambda b, t: (0, 0))
    return pl.pallas_call(
        _conv_prompt_kernel,
        grid=(batch, nt),
        in_specs=[
            pl.BlockSpec((tt, d), lambda b, t: (b * nt + t, 0)),
            pl.BlockSpec((CONV_WIDTH, d), lambda b, t: (0, 0)),
            vec, vec, vec,
        ],
        out_specs=pl.BlockSpec((tt, d), lambda b, t: (b * nt + t, 0)),
        out_shape=jax.ShapeDtypeStruct((m_rows, d), BF16),
        scratch_shapes=[pltpu.VMEM((d // LANES, CONV_HALO + tt, LANES), F32),
                        pltpu.VMEM((d // LANES, tt, LANES), F32)],
        compiler_params=_compiler_params(("parallel", "arbitrary")),
        name="conv_prompt",
    )(u, w_dw, b_dw, ln_g, ln_b)


def _conv_sample_kernel(cache_ref, u_ref, w_ref, b_ref, lg_ref, lb_ref, y_ref, state_ref, cv):
    hist, nseq, d = cache_ref.shape
    t_new = u_ref.shape[0]
    state_ref[0:hist - t_new] = cache_ref[t_new:hist]
    state_ref[hist - t_new:hist] = u_ref[...]

    def history_row(r, seqs, lanes):
        if r < hist:
            return cache_ref[r, seqs, lanes]
        return u_ref[r - hist, seqs, lanes]

    def lane_tile(lt, carry):
        lanes = pl.ds(pl.multiple_of(lt * LANES, LANES), LANES)
        bias = jnp.broadcast_to(b_ref[:, lanes], (SUBLANES, LANES))
        for s0 in range(0, nseq, SUBLANES):
            seqs = slice(s0, s0 + SUBLANES)
            accs = [bias] * t_new
            for j in range(CONV_WIDTH):
                wj = jnp.broadcast_to(w_ref[j:j + 1, lanes], (SUBLANES, LANES))
                for t in range(t_new):
                    accs[t] = accs[t] + wj * history_row(t + j, seqs, lanes)
            for t in range(t_new):
                cv[t, seqs, lanes] = accs[t]
        return carry

    lax.fori_loop(0, d // LANES, lane_tile, 0)
    y_ref[...] = _ln_silu(cv[...], lg_ref[...], lb_ref[...]).astype(BF16)


def _conv_sample(cache_t, u_t, w_dw, b_dw, ln_g, ln_b):
    hist, nb, d = cache_t.shape
    t_new = u_t.shape[0]
    nseq = SAMPLE_CONV_SEQS
    vec = pl.BlockSpec((1, d), lambda i: (0, 0))
    seq_block = lambda rows: pl.BlockSpec((rows, nseq, d), lambda i: (0, i, 0))
    return pl.pallas_call(
        _conv_sample_kernel,
        grid=(nb // nseq,),
        in_specs=[
            seq_block(hist), seq_block(t_new),
            pl.BlockSpec((CONV_WIDTH, d), lambda i: (0, 0)),
            vec, vec, vec,
        ],
        out_specs=[seq_block(t_new), seq_block(hist)],
        out_shape=[jax.ShapeDtypeStruct((t_new, nb, d), BF16),
                   jax.ShapeDtypeStruct((hist, nb, d), F32)],
        scratch_shapes=[pltpu.VMEM((t_new, nseq, d), F32)],
        compiler_params=_compiler_params(("parallel",)),
        name="conv_sample",
    )(cache_t, u_t, w_dw, b_dw, ln_g, ln_b)


def _proj_post_kernel(has_bias, *refs):
    if has_bias:
        a_ref, w_ref, b_ref, x_ref, gpost_ref, gnext_ref, xo_ref, ho_ref = refs
    else:
        a_ref, w_ref, x_ref, gpost_ref, gnext_ref, xo_ref, ho_ref = refs
    mix = _dot(a_ref[...], w_ref[...])
    if has_bias:
        mix = mix + b_ref[...]
    x1 = x_ref[...] + _rms(mix, gpost_ref[...])
    xo_ref[...] = x1
    ho_ref[...] = _rms(x1, gnext_ref[...]).astype(BF16)


def _proj_post(a, w, b, x, g_post, g_next):
    m_rows, k = a.shape
    d = w.shape[1]
    tm = ROW_TILE_FULL_N
    row = lambda width: pl.BlockSpec((tm, width), lambda m: (m, 0))
    vec = pl.BlockSpec((1, d), lambda m: (0, 0))
    has_bias = b is not None
    in_specs = [row(k), pl.BlockSpec((k, d), lambda m: (0, 0))]
    args = [a, w]
    if has_bias:
        in_specs.append(vec)
        args.append(b)
    in_specs += [row(d), vec, vec]
    args += [x, g_post, g_next]
    return pl.pallas_call(
        functools.partial(_proj_post_kernel, has_bias),
        grid=(m_rows // tm,),
        in_specs=in_specs,
        out_specs=[row(d), row(d)],
        out_shape=[jax.ShapeDtypeStruct((m_rows, d), F32), jax.ShapeDtypeStruct((m_rows, d), BF16)],
        compiler_params=_compiler_params(("parallel",)),
        name="proj_post",
    )(*args)


def _ffn_kernel(has_next, *refs):
    if has_next:
        h_ref, wg_ref, wu_ref, wd_ref, x_ref, gpost_ref, gnext_ref, xo_ref, ho_ref, acc = refs
    else:
        h_ref, wg_ref, wu_ref, wd_ref, x_ref, gpost_ref, xo_ref, acc = refs
    f = pl.program_id(1)

    @pl.when(f == 0)
    def _():
        acc[...] = jnp.zeros(acc.shape, F32)

    h = h_ref[...]
    gate = _dot(h, wg_ref[...])
    up = _dot(h, wu_ref[...])
    acc[...] += _dot((gate * _sigmoid(gate) * up).astype(BF16), wd_ref[...])

    @pl.when(f == pl.num_programs(1) - 1)
    def _():
        x2 = x_ref[...] + _rms(acc[...], gpost_ref[...])
        xo_ref[...] = x2
        if has_next:
            ho_ref[...] = _rms(x2, gnext_ref[...]).astype(BF16)


def _ffn(h, w_gate, w_up, w_down, x, g_post, g_next):
    m_rows, d = h.shape
    d_ff = w_gate.shape[1]
    tm, tf = ROW_TILE_FULL_N, 512
    row = pl.BlockSpec((tm, d), lambda m, f: (m, 0))
    vec = pl.BlockSpec((1, d), lambda m, f: (0, 0))
    has_next = g_next is not None
    in_specs = [
        row,
        pl.BlockSpec((d, tf), lambda m, f: (0, f)),
        pl.BlockSpec((d, tf), lambda m, f: (0, f)),
        pl.BlockSpec((tf, d), lambda m, f: (f, 0)),
        row, vec,
    ]
    args = [h, w_gate, w_up, w_down, x, g_post]
    out_specs = [row]
    out_shape = [jax.ShapeDtypeStruct((m_rows, d), F32)]
    if has_next:
        in_specs.append(vec)
        args.append(g_next)
        out_specs.append(row)
        out_shape.append(jax.ShapeDtypeStruct((m_rows, d), BF16))
    outs = pl.pallas_call(
        functools.partial(_ffn_kernel, has_next),
        grid=(m_rows // tm, d_ff // tf),
        in_specs=in_specs,
        out_specs=out_specs,
        out_shape=out_shape,
        scratch_shapes=[pltpu.VMEM((tm, d), F32)],
        compiler_params=_compiler_params(("parallel", "arbitrary")),
        name="ffn",
    )(*args)
    return (outs[0], outs[1]) if has_next else (outs[0], None)


def _qkvo_kernel(n_side, *refs):
    h_ref, wq_ref, wk_ref, wv_ref, wo_ref, wif_ref, bif_ref = refs[:7]
    side_in = refs[7:7 + n_side]
    q_ref, k_ref, v_ref, o_ref, g_ref = refs[7 + n_side:12 + n_side]
    side_out = refs[12 + n_side:12 + 2 * n_side]
    h = h_ref[...]
    q_ref[...] = (_dot(h, wq_ref[...]) * (DQK_HEAD ** -0.5)).astype(BF16)
    k_ref[...] = _dot(h, wk_ref[...]).astype(BF16)
    v_ref[...] = _dot(h, wv_ref[...]).astype(BF16)
    o_ref[...] = _sigmoid(_dot(h, wo_ref[...]))
    _run_side_casts(side_in, side_out)

    @pl.when(pl.program_id(1) == 0)
    def _():
        z = _dot(h, wif_ref[...]) + bif_ref[...]
        lane = lax.broadcasted_iota(jnp.int32, z.shape, 1)
        log_sig = jnp.minimum(z, 0.0) - jnp.log1p(jnp.exp(-jnp.abs(z)))
        g_ref[...] = jnp.where(lane < N_HEADS, z, log_sig)


def _qkvo(h, w_q, w_k, w_v, w_o, w_if, b_if, side_weights=()):
    m_rows, d = h.shape
    hk, hv = w_q.shape[1], w_v.shape[1]
    tm, steps = ROW_TILE, 4
    col = lambda width: pl.BlockSpec((d, width), lambda m, n: (0, n))
    out = lambda width: pl.BlockSpec((tm, width), lambda m, n: (m, n))
    side = _SideCasts(side_weights, (m_rows // tm) * steps, lambda m, n: m * steps + n)
    outs = pl.pallas_call(
        functools.partial(_qkvo_kernel, len(side)),
        grid=(m_rows // tm, steps),
        in_specs=[
            pl.BlockSpec((tm, d), lambda m, n: (m, 0)),
            col(hk // steps), col(hk // steps), col(hv // steps), col(hv // steps),
            pl.BlockSpec((d, LANES), lambda m, n: (0, 0)),
            pl.BlockSpec((1, LANES), lambda m, n: (0, 0)),
        ] + side.in_specs,
        out_specs=[out(hk // steps), out(hk // steps), out(hv // steps), out(hv // steps),
                   pl.BlockSpec((tm, LANES), lambda m, n: (m, 0))] + side.out_specs,
        out_shape=[
            jax.ShapeDtypeStruct((m_rows, hk), BF16),
            jax.ShapeDtypeStruct((m_rows, hk), BF16),
            jax.ShapeDtypeStruct((m_rows, hv), BF16),
            jax.ShapeDtypeStruct((m_rows, hv), F32),
            jax.ShapeDtypeStruct((m_rows, LANES), F32),
        ] + side.out_shapes,
        compiler_params=_compiler_params(("parallel", "arbitrary")),
        name="qkvo",
    )(h, w_q, w_k, w_v, w_o, w_if, b_if, *side.args)
    return tuple(outs[:5]), list(outs[5:])


def _cell_chunks(q, k, v, log_i, log_f, c_state, n_state, m_state):
    probs = range(len(q))
    length = q[0].shape[0]
    t_idx = lax.broadcasted_iota(jnp.int32, (length, length), 0)
    s_idx = lax.broadcasted_iota(jnp.int32, (length, length), 1)
    causal = s_idx <= t_idx
    eye = s_idx == t_idx
    nt_dims = (((1,), (1,)), ((), ()))
    tn_dims = (((0,), (0,)), ((), ()))

    qk = [lax.dot_general(q[p], k[p], nt_dims, preferred_element_type=F32) for p in probs]
    inter = [_dot(q[p], c_state[p].astype(BF16)) for p in probs]
    b_col = [jnp.sum(jnp.where(causal, log_f[p], 0.0), axis=1, keepdims=True) for p in probs]
    b_row = [jnp.sum(jnp.where(eye, b_col[p], 0.0), axis=0, keepdims=True) for p in probs]
    i_col = [jnp.sum(jnp.where(eye, log_i[p], 0.0), axis=1, keepdims=True) for p in probs]
    a_inter = [b_col[p] + m_state[p] for p in probs]
    d_intra = [jnp.where(causal, b_col[p] - b_row[p] + log_i[p], -jnp.inf) for p in probs]
    m_t = [jnp.maximum(a_inter[p], jnp.max(d_intra[p], axis=1, keepdims=True)) for p in probs]
    w_inter = [jnp.exp(a_inter[p] - m_t[p]) for p in probs]
    s = [qk[p] * jnp.exp(d_intra[p] - m_t[p]) for p in probs]
    intra = [_dot(s[p].astype(BF16), v[p]) for p in probs]

    def den(p):
        qn = w_inter[p] * (q[p].astype(F32) * n_state[p])
        if length % LANES == 0:
            for c0 in range(0, length, LANES):
                qn = qn + s[p][:, c0:c0 + LANES]
            return jnp.sum(qn, axis=1, keepdims=True)
        return jnp.sum(qn, axis=1, keepdims=True) + jnp.sum(s[p], axis=1, keepdims=True)

    h = [(w_inter[p] * inter[p] + intra[p]) / jnp.maximum(jnp.abs(den(p)), jnp.exp(-m_t[p]))
         for p in probs]

    m_new = [m_t[p][length - 1:length, :] for p in probs]
    b_last = [b_col[p][length - 1:length, :] for p in probs]
    g_state = [jnp.exp(b_last[p] + m_state[p] - m_new[p]) for p in probs]
    kw = [k[p].astype(F32) * jnp.exp(b_last[p] - b_col[p] + i_col[p] - m_new[p]) for p in probs]
    c_new = [g_state[p] * c_state[p]
             + lax.dot_general(kw[p].astype(BF16), v[p], tn_dims, preferred_element_type=F32)
             for p in probs]
    n_new = [g_state[p] * n_state[p] + jnp.sum(kw[p], axis=0, keepdims=True) for p in probs]
    return h, c_new, n_new, m_new


def _head_norm_gate(h, o, g_mh):
    hn = h * lax.rsqrt(jnp.mean(h * h, axis=-1, keepdims=True) + RMS_EPS)
    return o * (hn * g_mh)


def _cell_prompt_kernel(q_ref, k_ref, v_ref, o_ref, gt_ref, gmh_ref,
                        hg_ref, c_out, n_out, m_out, c_scr, n_scr, m_scr):
    c = pl.program_id(1)

    @pl.when(c == 0)
    def _():
        c_scr[...] = jnp.zeros(c_scr.shape, F32)
        n_scr[...] = jnp.zeros(n_scr.shape, F32)
        m_scr[...] = jnp.zeros(m_scr.shape, F32)

    heads = range(N_HEADS)
    qk_cols = [slice(hd * DQK_HEAD, (hd + 1) * DQK_HEAD) for hd in heads]
    v_cols = [slice(hd * DV_HEAD, (hd + 1) * DV_HEAD) for hd in heads]
    h, c_new, n_new, m_new = _cell_chunks(
        [q_ref[:, qk_cols[hd]] for hd in heads], [k_ref[:, qk_cols[hd]] for hd in heads],
        [v_ref[:, v_cols[hd]] for hd in heads],
        [gt_ref[hd:hd + 1, :] for hd in heads],
        [gt_ref[N_HEADS + hd:N_HEADS + hd + 1, :] for hd in heads],
        [c_scr[hd] for hd in heads], [n_scr[hd:hd + 1, :] for hd in heads],
        [m_scr[hd:hd + 1, 0:1] for hd in heads])
    for hd in heads:
        c_scr[hd] = c_new[hd]
        n_scr[hd:hd + 1, :] = n_new[hd]
        m_scr[hd:hd + 1, :] = jnp.broadcast_to(m_new[hd], (1, LANES))
        hg_ref[:, v_cols[hd]] = _head_norm_gate(
            h[hd], o_ref[:, v_cols[hd]], gmh_ref[:, v_cols[hd]]).astype(BF16)

    @pl.when(c == pl.num_programs(1) - 1)
    def _():
        c_out[0] = c_scr[...]
        n_out[0] = n_scr[...]
        m_out[0] = m_scr[...]


def _cell_prompt(q, k, v, o, gates_t, g_mh, batch):
    m_rows, hk = q.shape
    hv = v.shape[1]
    length = CELL_CHUNK
    nc = m_rows // batch // length
    rows = lambda width: pl.BlockSpec((length, width), lambda b, c: (b * nc + c, 0))
    return pl.pallas_call(
        _cell_prompt_kernel,
        grid=(batch, nc),
        in_specs=[
            rows(hk), rows(hk), rows(hv), rows(hv),
            pl.BlockSpec((2 * N_HEADS, length), lambda b, c: (0, b * nc + c)),
            pl.BlockSpec((1, hv), lambda b, c: (0, 0)),
        ],
        out_specs=[
            rows(hv),
            pl.BlockSpec((1, N_HEADS, DQK_HEAD, DV_HEAD), lambda b, c: (b, 0, 0, 0)),
            pl.BlockSpec((1, N_HEADS, DQK_HEAD), lambda b, c: (b, 0, 0)),
            pl.BlockSpec((1, N_HEADS, LANES), lambda b, c: (b, 0, 0)),
        ],
        out_shape=[
            jax.ShapeDtypeStruct((m_rows, hv), BF16),
            jax.ShapeDtypeStruct((batch, N_HEADS, DQK_HEAD, DV_HEAD), F32),
            jax.ShapeDtypeStruct((batch, N_HEADS, DQK_HEAD), F32),
            jax.ShapeDtypeStruct((batch, N_HEADS, LANES), F32),
        ],
        scratch_shapes=[
            pltpu.VMEM((N_HEADS, DQK_HEAD, DV_HEAD), F32),
            pltpu.VMEM((N_HEADS, DQK_HEAD), F32),
            pltpu.VMEM((N_HEADS, LANES), F32),
        ],
        compiler_params=_compiler_params(("parallel", "arbitrary")),
        name="cell_prompt",
    )(q, k, v, o, gates_t, g_mh)


def _cell_sample_kernel(q_ref, k_ref, v_ref, o_ref, gt_ref, gmh_ref, c_in, n_in, m_in,
                        hg_ref, c_out, n_out, m_out):
    t_new = o_ref.shape[1]

    heads = range(N_HEADS)
    qk_cols = [slice(hd * DQK_HEAD, (hd + 1) * DQK_HEAD) for hd in heads]
    v_cols = [slice(hd * DV_HEAD, (hd + 1) * DV_HEAD) for hd in heads]

    def sequence_group(g, carry):
        probs = [(g * SAMPLE_UNROLL + u, hd) for u in range(SAMPLE_UNROLL) for hd in heads]
        h, c_new, n_new, m_new = _cell_chunks(
            [q_ref[b, :, qk_cols[hd]] for b, hd in probs], [k_ref[b, :, qk_cols[hd]] for b, hd in probs],
            [v_ref[b, :, v_cols[hd]] for b, hd in probs],
            [gt_ref[b, hd:hd + 1, :] for b, hd in probs],
            [gt_ref[b, N_HEADS + hd:N_HEADS + hd + 1, :] for b, hd in probs],
            [c_in[b, hd] for b, hd in probs], [n_in[b, hd:hd + 1, :] for b, hd in probs],
            [m_in[b, hd:hd + 1, 0:1] for b, hd in probs])
        for p, (b, hd) in enumerate(probs):
            c_out[b, hd] = c_new[p]
            n_out[b, hd:hd + 1, :] = n_new[p]
            m_out[b, hd:hd + 1, :] = jnp.broadcast_to(m_new[p], (1, LANES))
            hg_ref[b, :, v_cols[hd]] = _head_norm_gate(
                h[p][0:t_new, :], o_ref[b, :, v_cols[hd]], gmh_ref[:, v_cols[hd]])
        return carry

    lax.fori_loop(0, q_ref.shape[0] // SAMPLE_UNROLL, sequence_group, 0)


def _cell_sample(q, k, v, o, gates_t, g_mh, c0, n0, m0):
    nb, tp, hk = q.shape
    hv = v.shape[2]
    t_new = o.shape[1]
    nseq = SAMPLE_SEQS
    blk3 = lambda s1, s2: pl.BlockSpec((nseq, s1, s2), lambda i: (i, 0, 0))
    c_spec = pl.BlockSpec((nseq, N_HEADS, DQK_HEAD, DV_HEAD), lambda i: (i, 0, 0, 0))
    return pl.pallas_call(
        _cell_sample_kernel,
        grid=(nb // nseq,),
        in_specs=[
            blk3(tp, hk), blk3(tp, hk), blk3(tp, hv), blk3(t_new, hv),
            blk3(2 * N_HEADS, tp),
            pl.BlockSpec((1, hv), lambda i: (0, 0)),
            c_spec, blk3(N_HEADS, DQK_HEAD), blk3(N_HEADS, LANES),
        ],
        out_specs=[blk3(t_new, hv), c_spec, blk3(N_HEADS, DQK_HEAD), blk3(N_HEADS, LANES)],
        out_shape=[
            jax.ShapeDtypeStruct((nb, t_new, hv), F32),
            jax.ShapeDtypeStruct((nb, N_HEADS, DQK_HEAD, DV_HEAD), F32),
            jax.ShapeDtypeStruct((nb, N_HEADS, DQK_HEAD), F32),
            jax.ShapeDtypeStruct((nb, N_HEADS, LANES), F32),
        ],
        compiler_params=_compiler_params(("parallel",)),
        name="cell_sample",
    )(q, k, v, o, gates_t, g_mh, c0, n0, m0)


def _row(v):
    return v.reshape(1, -1).astype(F32)


def _swap_row_order(a, outer, inner):
    return jnp.swapaxes(a.reshape(outer, inner, -1), 0, 1).reshape(outer * inner, -1)


def _conv_layer(x, batch, past, p, side_weights):
    m_rows, d = x.shape
    u, side = _pw1_glu(x, p["g_pre"], p["w_pw1"], p["b_pw1"], side_weights)
    if past is None:
        y = _conv_prompt(u, batch, p["w_dw"], p["b_dw"], p["ln_g"], p["ln_b"])
        state = u.reshape(batch, -1, d)[:, -CONV_STATE:]
    else:
        y_t, state_t = _conv_sample(
            jnp.swapaxes(past.astype(F32), 0, 1), u.reshape(m_rows // batch, batch, d),
            p["w_dw"], p["b_dw"], p["ln_g"], p["ln_b"])
        y = y_t.reshape(m_rows, d)
        state = jnp.swapaxes(state_t, 0, 1)
    return y, state, side


def _mlstm_layer(h, batch, state, p, side_weights):
    m_rows = h.shape[0]
    t_len = m_rows // batch
    (q, k, v, o, gates), side = _qkvo(
        h, p["w_q"], p["w_k"], p["w_v"], p["w_o"], p["w_if"], p["b_if"], side_weights)
    gates = gates[:, :2 * N_HEADS]
    if state is None:
        hg, c_new, n_new, m_new = _cell_prompt(q, k, v, o, gates.T, p["g_mh"], batch)
    else:
        c0, n0, m0 = state
        pad_t = SAMPLE_PAD_T - t_len
        pad3 = lambda a: jnp.pad(a.reshape(batch, t_len, -1), ((0, 0), (0, pad_t), (0, 0)))
        g3 = gates.reshape(batch, t_len, 2 * N_HEADS)
        g_pad = jnp.concatenate(
            [jnp.full((batch, pad_t, N_HEADS), -jnp.inf, F32), jnp.zeros((batch, pad_t, N_HEADS), F32)], axis=2)
        gates_t = jnp.swapaxes(jnp.concatenate([g3, g_pad], axis=1), 1, 2)
        m0_lanes = jnp.broadcast_to(m0.astype(F32)[:, :, None], (batch, N_HEADS, LANES))
        hg, c_new, n_new, m_new = _cell_sample(
            pad3(q), pad3(k), pad3(v), o.reshape(batch, t_len, -1), gates_t, p["g_mh"],
            c0.astype(F32), n0.astype(F32), m0_lanes)
        hg = hg.reshape(m_rows, -1).astype(BF16)
    return hg, (c_new, n_new, m_new[:, :, 0]), side


def kernel(x_prompt, x_sample, cache_conv, state_C, state_n, state_m, conv_w_pw1, conv_b_pw1, conv_w_dw, conv_b_dw, conv_ln_g, conv_ln_b, conv_w_pw2, conv_b_pw2, mlstm_w_q, mlstm_w_k, mlstm_w_v, mlstm_w_o, mlstm_w_i, mlstm_b_i, mlstm_w_f, mlstm_b_f, mlstm_g_norm, mlstm_w_out, ffn_w_gate, ffn_w_up, ffn_w_down, norm_mix_pre, norm_mix_post, norm_ffn_pre, norm_ffn_post):
    depth = ffn_w_gate.shape[0]
    d = x_prompt.shape[-1]
    batches = (x_prompt.shape[0], x_sample.shape[0])
    xs = [x_prompt.reshape(-1, d), x_sample.reshape(-1, d)]
    hs = [None, None]
    conv_new = [[], []]
    c_new, n_new, m_new = [[], []], [[], []], [[], []]
    sample_batch, sample_t = x_sample.shape[0], x_sample.shape[1]
    sample_time_major = False

    for i in range(depth):
        j = i // 2
        is_conv = i % 2 == 0
        if is_conv:
            p = {
                "g_pre": _row(norm_mix_pre[i]),
                "w_pw1": conv_w_pw1[j].astype(BF16), "b_pw1": _row(conv_b_pw1[j]),
                "w_dw": conv_w_dw[j].astype(F32), "b_dw": _row(conv_b_dw[j]),
                "ln_g": _row(conv_ln_g[j]), "ln_b": _row(conv_ln_b[j]),
            }
            w_proj_all, b_proj = conv_w_pw2, _row(conv_b_pw2[j])
        else:
            w_if = jnp.concatenate([mlstm_w_i[j], mlstm_w_f[j]], axis=1)
            b_if = jnp.concatenate([mlstm_b_i[j], mlstm_b_f[j]], axis=0)
            pad = LANES - w_if.shape[1]
            p = {
                "w_q": mlstm_w_q[j].astype(BF16), "w_k": mlstm_w_k[j].astype(BF16),
                "w_v": mlstm_w_v[j].astype(BF16), "w_o": mlstm_w_o[j].astype(BF16),
                "w_if": jnp.pad(w_if, ((0, 0), (0, pad))).astype(BF16),
                "b_if": jnp.pad(_row(b_if), ((0, 0), (0, pad))),
                "g_mh": _row(mlstm_g_norm[j]),
            }
            w_proj_all, b_proj = mlstm_w_out, None
        next_is_mlstm = i + 1 < depth and (i + 1) % 2 == 1
        g_next = _row(norm_mix_pre[i + 1]) if next_is_mlstm else None

        if sample_time_major != is_conv:
            dims = (sample_t, sample_batch) if sample_time_major else (sample_batch, sample_t)
            xs[1] = _swap_row_order(xs[1], *dims)
            hs[1] = None if hs[1] is None else _swap_row_order(hs[1], *dims)
            sample_time_major = is_conv

        to_cast = [(ffn_w_gate, i), (ffn_w_up, i), (ffn_w_down, i), (w_proj_all, j)]

        for gi in range(2):
            x, batch = xs[gi], batches[gi]
            side_weights = to_cast if gi == 0 else ()
            if is_conv:
                past = None if gi == 0 else cache_conv[j]
                mix_in, state, side = _conv_layer(x, batch, past, p, side_weights)
                conv_new[gi].append(state.astype(cache_conv.dtype))
            else:
                state = None if gi == 0 else (state_C[j], state_n[j], state_m[j])
                mix_in, (c1, n1, m1), side = _mlstm_layer(hs[gi], batch, state, p, side_weights)
                c_new[gi].append(c1.astype(state_C.dtype))
                n_new[gi].append(n1.astype(state_n.dtype))
                m_new[gi].append(m1.astype(state_m.dtype))
            if gi == 0:
                w_gate, w_up, w_down, w_proj = side
            x, h_ffn = _proj_post(mix_in, w_proj, b_proj, x, _row(norm_mix_post[i]), _row(norm_ffn_pre[i]))
            xs[gi], hs[gi] = _ffn(h_ffn, w_gate, w_up, w_down, x, _row(norm_ffn_post[i]), g_next)

    if sample_time_major:
        xs[1] = _swap_row_order(xs[1], sample_t, sample_batch)

    return (xs[0].reshape(x_prompt.shape), xs[1].reshape(x_sample.shape),
            jnp.stack(conv_new[0]), jnp.stack(conv_new[1]),
            jnp.stack(c_new[0]), jnp.stack(n_new[0]), jnp.stack(m_new[0]),
            jnp.stack(c_new[1]), jnp.stack(n_new[1]), jnp.stack(m_new[1]))
```

```python
import functools

import jax
import jax.numpy as jnp
from jax import lax
from jax.experimental import pallas as pl
from jax.experimental.pallas import tpu as pltpu

F32 = jnp.float32
BF16 = jnp.bfloat16

D_MODEL = 2048
N_HEADS = 8
DV_HEAD = D_MODEL // N_HEADS
DQK_HEAD = DV_HEAD // 2
CONV_WIDTH = 31
CONV_STATE = CONV_WIDTH - 1
RMS_EPS = 1e-6
LN_EPS = 1e-5

LANES = 128
SUBLANES = 8
BF16_ROWS = 16
MIB = 1024 * 1024
VMEM_LIMIT = 52 * MIB

ROW_TILE = 1024
ROW_TILE_FULL_N = 512
PROJ_SUBTILES = 4
FFN_ROW_TILE = 1024
FFN_EPILOGUE_ROWS = 128
FFN_VMEM_LIMIT = 56 * MIB
CONV_TIME_TILE = 256
CONV_HALO = 32
CONV_ROW_STRIDE = 4
CELL_CHUNK = 256
SAMPLE_PAD_T = BF16_ROWS
SAMPLE_SEQS = 4
SAMPLE_UNROLL = 2
SAMPLE_CONV_SEQS = 16


def _compiler_params(semantics):
    return pltpu.CompilerParams(dimension_semantics=semantics, vmem_limit_bytes=VMEM_LIMIT)


def _dot(a, b):
    return jnp.dot(a, b, preferred_element_type=F32)


def _rms(x, g):
    return x * lax.rsqrt(jnp.mean(x * x, axis=-1, keepdims=True) + RMS_EPS) * g


def _sigmoid(x):
    return jax.nn.sigmoid(x)


class _SideCasts:
    def __init__(self, weights, steps, step_of):
        self.args, self.in_specs, self.out_specs, self.out_shapes = [], [], [], []
        for w, idx in weights:
            _, r, c = w.shape
            share = 1
            while (r * share) % steps or (r * share // steps) % BF16_ROWS:
                share *= 2
            rows = r * share // steps
            self.args.append(w)
            self.in_specs.append(pl.BlockSpec(
                (None, rows, c), lambda *g, idx=idx, share=share: (idx, step_of(*g) // share, 0)))
            self.out_specs.append(pl.BlockSpec(
                (rows, c), lambda *g, share=share: (step_of(*g) // share, 0)))
            self.out_shapes.append(jax.ShapeDtypeStruct((r, c), BF16))

    def __len__(self):
        return len(self.args)


def _run_side_casts(src_refs, dst_refs):
    for src, dst in zip(src_refs, dst_refs):
        dst[...] = src[...].astype(BF16)


def _pw1_glu_kernel(x_ref, g_ref, wa_ref, wg_ref, ba_ref, bg_ref, u_ref, h_scr):
    @pl.when(pl.program_id(1) == 0)
    def _():
        h_scr[...] = _rms(x_ref[...], g_ref[...]).astype(BF16)

    h = h_scr[...]
    a = _dot(h, wa_ref[...]) + ba_ref[...]
    g = _dot(h, wg_ref[...]) + bg_ref[...]
    u_ref[...] = a * _sigmoid(g)


def _pw1_glu(x, g_pre, w, b):
    m_rows, d = x.shape
    tm, tn = ROW_TILE, 512
    nn = d // tn
    return pl.pallas_call(
        _pw1_glu_kernel,
        grid=(m_rows // tm, nn),
        in_specs=[
            pl.BlockSpec((tm, d), lambda m, n: (m, 0)),
            pl.BlockSpec((1, d), lambda m, n: (0, 0)),
            pl.BlockSpec((d, tn), lambda m, n: (0, n)),
            pl.BlockSpec((d, tn), lambda m, n: (0, n + nn)),
            pl.BlockSpec((1, tn), lambda m, n: (0, n)),
            pl.BlockSpec((1, tn), lambda m, n: (0, n + nn)),
        ],
        out_specs=pl.BlockSpec((tm, tn), lambda m, n: (m, n)),
        out_shape=jax.ShapeDtypeStruct((m_rows, d), F32),
        scratch_shapes=[pltpu.VMEM((tm, d), BF16)],
        compiler_params=_compiler_params(("parallel", "arbitrary")),
        name="pw1_glu",
    )(x, g_pre, w, w, b, b)


def _ln_silu(v, ln_g, ln_b):
    xc = v - jnp.mean(v, axis=-1, keepdims=True)
    var = jnp.mean(xc * xc, axis=-1, keepdims=True)
    y = xc * lax.rsqrt(var + LN_EPS) * ln_g + ln_b
    return y * _sigmoid(y)


def _conv_prompt_kernel(n_side, *refs):
    u_ref, w_ref, b_ref, lg_ref, lb_ref = refs[:5]
    side_in = refs[5:5 + n_side]
    y_ref = refs[5 + n_side]
    side_out = refs[6 + n_side:6 + 2 * n_side]
    ext, cv = refs[6 + 2 * n_side:]
    _run_side_casts(side_in, side_out)
    tt, d = u_ref.shape
    nl = d // LANES
    first_tap = CONV_HALO - CONV_STATE
    block_rows = SUBLANES * CONV_ROW_STRIDE

    @pl.when(pl.program_id(1) == 0)
    def _():
        ext[:, 0:CONV_HALO, :] = jnp.zeros((nl, CONV_HALO, LANES), F32)

    @pl.when(pl.program_id(1) != 0)
    def _():
        ext[:, 0:CONV_HALO, :] = ext[:, tt:tt + CONV_HALO, :]

    for lt in range(nl):
        ext[lt, CONV_HALO:CONV_HALO + tt, :] = u_ref[:, lt * LANES:(lt + 1) * LANES]

    def lane_tile(lt, carry):
        lanes = pl.ds(pl.multiple_of(lt * LANES, LANES), LANES)
        bias = jnp.broadcast_to(b_ref[:, lanes], (SUBLANES, LANES))
        for r0 in range(0, tt, block_rows):
            accs = [bias] * CONV_ROW_STRIDE
            window = {}
            for j in range(CONV_WIDTH):
                wj = jnp.broadcast_to(w_ref[j:j + 1, lanes], (SUBLANES, LANES))
                for s in range(CONV_ROW_STRIDE):
                    if s + j not in window:
                        window[s + j] = ext[lt, pl.ds(r0 + first_tap + s + j, SUBLANES, stride=CONV_ROW_STRIDE), :]
                    accs[s] = accs[s] + wj * window[s + j]
            for s in range(CONV_ROW_STRIDE):
                cv[lt, pl.ds(r0 + s, SUBLANES, stride=CONV_ROW_STRIDE), :] = accs[s]
        return carry

    lax.fori_loop(0, nl, lane_tile, 0)

    v = cv[...]
    xc = v - jnp.sum(jnp.sum(v, axis=0), axis=-1, keepdims=True) / d
    var = jnp.sum(jnp.sum(xc * xc, axis=0), axis=-1, keepdims=True) / d
    inv = lax.rsqrt(var + LN_EPS)
    for lt in range(nl):
        cols = slice(lt * LANES, (lt + 1) * LANES)
        y = xc[lt] * inv * lg_ref[:, cols] + lb_ref[:, cols]
        y_ref[:, cols] = (y * _sigmoid(y)).astype(BF16)


def _conv_prompt(u, batch, w_dw, b_dw, ln_g, ln_b, side_weights):
    m_rows, d = u.shape
    tt = CONV_TIME_TILE
    nt = m_rows // batch // tt
    vec = pl.BlockSpec((1, d), lambda b, t: (0, 0))
    side = _SideCasts(side_weights, batch * nt, lambda b, t: b * nt + t)
    outs = pl.pallas_call(
        functools.partial(_conv_prompt_kernel, len(side)),
        grid=(batch, nt),
        in_specs=[
            pl.BlockSpec((tt, d), lambda b, t: (b * nt + t, 0)),
            pl.BlockSpec((CONV_WIDTH, d), lambda b, t: (0, 0)),
            vec, vec, vec,
        ] + side.in_specs,
        out_specs=[pl.BlockSpec((tt, d), lambda b, t: (b * nt + t, 0))] + side.out_specs,
        out_shape=[jax.ShapeDtypeStruct((m_rows, d), BF16)] + side.out_shapes,
        scratch_shapes=[pltpu.VMEM((d // LANES, CONV_HALO + tt, LANES), F32),
                        pltpu.VMEM((d // LANES, tt, LANES), F32)],
        compiler_params=_compiler_params(("parallel", "arbitrary")),
        name="conv_prompt",
    )(u, w_dw, b_dw, ln_g, ln_b, *side.args)
    return outs[0], list(outs[1:])


def _conv_sample_kernel(cache_ref, u_ref, w_ref, b_ref, lg_ref, lb_ref, y_ref, state_ref, cv):
    hist, nseq, d = cache_ref.shape
    t_new = u_ref.shape[0]
    state_ref[0:hist - t_new] = cache_ref[t_new:hist]
    state_ref[hist - t_new:hist] = u_ref[...]

    def history_row(r, seqs, lanes):
        if r < hist:
            return cache_ref[r, seqs, lanes]
        return u_ref[r - hist, seqs, lanes]

    def lane_tile(lt, carry):
        lanes = pl.ds(pl.multiple_of(lt * LANES, LANES), LANES)
        bias = jnp.broadcast_to(b_ref[:, lanes], (SUBLANES, LANES))
        for s0 in range(0, nseq, SUBLANES):
            seqs = slice(s0, s0 + SUBLANES)
            accs = [bias] * t_new
            for j in range(CONV_WIDTH):
                wj = jnp.broadcast_to(w_ref[j:j + 1, lanes], (SUBLANES, LANES))
                for t in range(t_new):
                    accs[t] = accs[t] + wj * history_row(t + j, seqs, lanes)
            for t in range(t_new):
                cv[t, seqs, lanes] = accs[t]
        return carry

    lax.fori_loop(0, d // LANES, lane_tile, 0)
    y_ref[...] = _ln_silu(cv[...], lg_ref[...], lb_ref[...]).astype(BF16)


def _conv_sample(cache_t, u_t, w_dw, b_dw, ln_g, ln_b):
    hist, nb, d = cache_t.shape
    t_new = u_t.shape[0]
    nseq = SAMPLE_CONV_SEQS
    vec = pl.BlockSpec((1, d), lambda i: (0, 0))
    seq_block = lambda rows: pl.BlockSpec((rows, nseq, d), lambda i: (0, i, 0))
    return pl.pallas_call(
        _conv_sample_kernel,
        grid=(nb // nseq,),
        in_specs=[
            seq_block(hist), seq_block(t_new),
            pl.BlockSpec((CONV_WIDTH, d), lambda i: (0, 0)),
            vec, vec, vec,
        ],
        out_specs=[seq_block(t_new), seq_block(hist)],
        out_shape=[jax.ShapeDtypeStruct((t_new, nb, d), BF16),
                   jax.ShapeDtypeStruct((hist, nb, d), F32)],
        scratch_shapes=[pltpu.VMEM((t_new, nseq, d), F32)],
        compiler_params=_compiler_params(("parallel",)),
        name="conv_sample",
    )(cache_t, u_t, w_dw, b_dw, ln_g, ln_b)


def _proj_post_kernel(has_bias, *refs):
    if has_bias:
        a_ref, w_ref, b_ref, x_ref, gpost_ref, gnext_ref, xo_ref, ho_ref = refs
    else:
        a_ref, w_ref, x_ref, gpost_ref, gnext_ref, xo_ref, ho_ref = refs
    w = w_ref[...]
    sub = a_ref.shape[0] // PROJ_SUBTILES

    def finish(i, mix):
        rows = slice(i * sub, (i + 1) * sub)
        if has_bias:
            mix = mix + b_ref[...]
        x1 = x_ref[rows, :] + _rms(mix, gpost_ref[...])
        xo_ref[rows, :] = x1
        ho_ref[rows, :] = _rms(x1, gnext_ref[...]).astype(BF16)

    pending = None
    for i in range(PROJ_SUBTILES):
        mix = _dot(a_ref[i * sub:(i + 1) * sub, :], w)
        if pending is not None:
            finish(i - 1, pending)
        pending = mix
    finish(PROJ_SUBTILES - 1, pending)


def _proj_post(a, w, b, x, g_post, g_next):
    m_rows, k = a.shape
    d = w.shape[1]
    tm = ROW_TILE_FULL_N
    row = lambda width: pl.BlockSpec((tm, width), lambda m: (m, 0))
    vec = pl.BlockSpec((1, d), lambda m: (0, 0))
    has_bias = b is not None
    in_specs = [row(k), pl.BlockSpec((k, d), lambda m: (0, 0))]
    args = [a, w]
    if has_bias:
        in_specs.append(vec)
        args.append(b)
    in_specs += [row(d), vec, vec]
    args += [x, g_post, g_next]
    return pl.pallas_call(
        functools.partial(_proj_post_kernel, has_bias),
        grid=(m_rows // tm,),
        in_specs=in_specs,
        out_specs=[row(d), row(d)],
        out_shape=[jax.ShapeDtypeStruct((m_rows, d), F32), jax.ShapeDtypeStruct((m_rows, d), BF16)],
        compiler_params=_compiler_params(("parallel",)),
        name="proj_post",
    )(*args)


def _ffn_kernel(has_next, *refs):
    if has_next:
        h_ref, wg_ref, wu_ref, wd_ref, x_hbm, gpost_ref, gnext_ref, xo_ref, ho_ref, x_buf, x_sem = refs
    else:
        h_ref, wg_ref, wu_ref, wd_ref, x_hbm, gpost_ref, xo_ref, x_buf, x_sem = refs
    f = pl.program_id(1)
    tm = xo_ref.shape[0]
    row0 = pl.multiple_of(pl.program_id(0) * tm, tm)
    x_copy = pltpu.make_async_copy(x_hbm.at[pl.ds(row0, tm), :], x_buf, x_sem)

    @pl.when(f == 0)
    def _():
        x_copy.start()
        xo_ref[...] = jnp.zeros(xo_ref.shape, F32)

    h = h_ref[...]
    gate = _dot(h, wg_ref[...])
    up = _dot(h, wu_ref[...])
    xo_ref[...] += _dot((gate * _sigmoid(gate) * up).astype(BF16), wd_ref[...])

    @pl.when(f == pl.num_programs(1) - 1)
    def _():
        x_copy.wait()
        for r0 in range(0, tm, FFN_EPILOGUE_ROWS):
            rows = slice(r0, r0 + FFN_EPILOGUE_ROWS)
            x2 = x_buf[rows, :] + _rms(xo_ref[rows, :], gpost_ref[...])
            xo_ref[rows, :] = x2
            if has_next:
                ho_ref[rows, :] = _rms(x2, gnext_ref[...]).astype(BF16)


def _ffn(h, w_gate, w_up, w_down, x, g_post, g_next):
    m_rows, d = h.shape
    d_ff = w_gate.shape[1]
    tm, tf = FFN_ROW_TILE, 512
    row = pl.BlockSpec((tm, d), lambda m, f: (m, 0))
    vec = pl.BlockSpec((1, d), lambda m, f: (0, 0))
    has_next = g_next is not None
    in_specs = [
        pl.BlockSpec((tm, d), lambda m, f: (m, 0), pipeline_mode=pl.Buffered(1)),
        pl.BlockSpec((d, tf), lambda m, f: (0, f)),
        pl.BlockSpec((d, tf), lambda m, f: (0, f)),
        pl.BlockSpec((tf, d), lambda m, f: (f, 0)),
        pl.BlockSpec(memory_space=pl.ANY), vec,
    ]
    args = [h, w_gate, w_up, w_down, x, g_post]
    out_specs = [row]
    out_shape = [jax.ShapeDtypeStruct((m_rows, d), F32)]
    if has_next:
        in_specs.append(vec)
        args.append(g_next)
        out_specs.append(row)
        out_shape.append(jax.ShapeDtypeStruct((m_rows, d), BF16))
    outs = pl.pallas_call(
        functools.partial(_ffn_kernel, has_next),
        grid=(m_rows // tm, d_ff // tf),
        in_specs=in_specs,
        out_specs=out_specs,
        out_shape=out_shape,
        scratch_shapes=[pltpu.VMEM((tm, d), F32), pltpu.SemaphoreType.DMA(())],
        compiler_params=pltpu.CompilerParams(
            dimension_semantics=("parallel", "arbitrary"), vmem_limit_bytes=FFN_VMEM_LIMIT),
        name="ffn",
    )(*args)
    return (outs[0], outs[1]) if has_next else (outs[0], None)


def _qkvo_kernel(h_ref, wq_ref, wk_ref, wv_ref, wo_ref, wif_ref, bif_ref,
                 q_ref, k_ref, v_ref, o_ref, g_ref):
    h = h_ref[...]
    q_ref[...] = (_dot(h, wq_ref[...]) * (DQK_HEAD ** -0.5)).astype(BF16)
    k_ref[...] = _dot(h, wk_ref[...]).astype(BF16)
    v_ref[...] = _dot(h, wv_ref[...]).astype(BF16)
    o_ref[...] = _sigmoid(_dot(h, wo_ref[...]))

    @pl.when(pl.program_id(1) == 0)
    def _():
        z = _dot(h, wif_ref[...]) + bif_ref[...]
        lane = lax.broadcasted_iota(jnp.int32, z.shape, 1)
        log_sig = jnp.minimum(z, 0.0) - jnp.log1p(jnp.exp(-jnp.abs(z)))
        g_ref[...] = jnp.where(lane < N_HEADS, z, log_sig)


def _qkvo(h, w_q, w_k, w_v, w_o, w_if, b_if):
    m_rows, d = h.shape
    hk, hv = w_q.shape[1], w_v.shape[1]
    tm, steps = ROW_TILE, 4
    col = lambda width: pl.BlockSpec((d, width), lambda m, n: (0, n))
    out = lambda width: pl.BlockSpec((tm, width), lambda m, n: (m, n))
    return pl.pallas_call(
        _qkvo_kernel,
        grid=(m_rows // tm, steps),
        in_specs=[
            pl.BlockSpec((tm, d), lambda m, n: (m, 0)),
            col(hk // steps), col(hk // steps), col(hv // steps), col(hv // steps),
            pl.BlockSpec((d, LANES), lambda m, n: (0, 0)),
            pl.BlockSpec((1, LANES), lambda m, n: (0, 0)),
        ],
        out_specs=[out(hk // steps), out(hk // steps), out(hv // steps), out(hv // steps),
                   pl.BlockSpec((tm, LANES), lambda m, n: (m, 0))],
        out_shape=[
            jax.ShapeDtypeStruct((m_rows, hk), BF16),
            jax.ShapeDtypeStruct((m_rows, hk), BF16),
            jax.ShapeDtypeStruct((m_rows, hv), BF16),
            jax.ShapeDtypeStruct((m_rows, hv), F32),
            jax.ShapeDtypeStruct((m_rows, LANES), F32),
        ],
        compiler_params=_compiler_params(("parallel", "arbitrary")),
        name="qkvo",
    )(h, w_q, w_k, w_v, w_o, w_if, b_if)


def _cell_chunks(q, k, v, log_i, log_f, c_state, n_state, m_state):
    probs = range(len(q))
    length = q[0].shape[0]
    t_idx = lax.broadcasted_iota(jnp.int32, (length, length), 0)
    s_idx = lax.broadcasted_iota(jnp.int32, (length, length), 1)
    causal = s_idx <= t_idx
    eye = s_idx == t_idx
    nt_dims = (((1,), (1,)), ((), ()))
    tn_dims = (((0,), (0,)), ((), ()))

    qk = [lax.dot_general(q[p], k[p], nt_dims, preferred_element_type=F32) for p in probs]
    inter = [_dot(q[p], c_state[p].astype(BF16)) for p in probs]
    b_col = [jnp.sum(jnp.where(causal, log_f[p], 0.0), axis=1, keepdims=True) for p in probs]
    b_row = [jnp.sum(jnp.where(eye, b_col[p], 0.0), axis=0, keepdims=True) for p in probs]
    i_col = [jnp.sum(jnp.where(eye, log_i[p], 0.0), axis=1, keepdims=True) for p in probs]
    a_inter = [b_col[p] + m_state[p] for p in probs]
    d_intra = [jnp.where(causal, b_col[p] - b_row[p] + log_i[p], -jnp.inf) for p in probs]
    m_t = [jnp.maximum(a_inter[p], jnp.max(d_intra[p], axis=1, keepdims=True)) for p in probs]
    w_inter = [jnp.exp(a_inter[p] - m_t[p]) for p in probs]
    s = [qk[p] * jnp.exp(d_intra[p] - m_t[p]) for p in probs]
    intra = [_dot(s[p].astype(BF16), v[p]) for p in probs]

    def den(p):
        qn = w_inter[p] * (q[p].astype(F32) * n_state[p])
        if length % LANES == 0:
            for c0 in range(0, length, LANES):
                qn = qn + s[p][:, c0:c0 + LANES]
            return jnp.sum(qn, axis=1, keepdims=True)
        return jnp.sum(qn, axis=1, keepdims=True) + jnp.sum(s[p], axis=1, keepdims=True)

    h = [(w_inter[p] * inter[p] + intra[p]) / jnp.maximum(jnp.abs(den(p)), jnp.exp(-m_t[p]))
         for p in probs]

    m_new = [m_t[p][length - 1:length, :] for p in probs]
    b_last = [b_col[p][length - 1:length, :] for p in probs]
    g_state = [jnp.exp(b_last[p] + m_state[p] - m_new[p]) for p in probs]
    kw = [k[p].astype(F32) * jnp.exp(b_last[p] - b_col[p] + i_col[p] - m_new[p]) for p in probs]
    c_new = [g_state[p] * c_state[p]
             + lax.dot_general(kw[p].astype(BF16), v[p], tn_dims, preferred_element_type=F32)
             for p in probs]
    n_new = [g_state[p] * n_state[p] + jnp.sum(kw[p], axis=0, keepdims=True) for p in probs]
    return h, c_new, n_new, m_new


def _head_norm_gate(h, o, g_mh):
    hn = h * lax.rsqrt(jnp.mean(h * h, axis=-1, keepdims=True) + RMS_EPS)
    return o * (hn * g_mh)


def _cell_prompt_kernel(n_side, *refs):
    q_ref, k_ref, v_ref, o_ref, gt_ref, gmh_ref = refs[:6]
    side_in = refs[6:6 + n_side]
    hg_ref, c_out, n_out, m_out = refs[6 + n_side:10 + n_side]
    side_out = refs[10 + n_side:10 + 2 * n_side]
    c_scr, n_scr, m_scr = refs[10 + 2 * n_side:]
    _run_side_casts(side_in, side_out)
    c = pl.program_id(1)

    @pl.when(c == 0)
    def _():
        c_scr[...] = jnp.zeros(c_scr.shape, F32)
        n_scr[...] = jnp.zeros(n_scr.shape, F32)
        m_scr[...] = jnp.zeros(m_scr.shape, F32)

    heads = range(N_HEADS)
    qk_cols = [slice(hd * DQK_HEAD, (hd + 1) * DQK_HEAD) for hd in heads]
    v_cols = [slice(hd * DV_HEAD, (hd + 1) * DV_HEAD) for hd in heads]
    h, c_new, n_new, m_new = _cell_chunks(
        [q_ref[:, qk_cols[hd]] for hd in heads], [k_ref[:, qk_cols[hd]] for hd in heads],
        [v_ref[:, v_cols[hd]] for hd in heads],
        [gt_ref[hd:hd + 1, :] for hd in heads],
        [gt_ref[N_HEADS + hd:N_HEADS + hd + 1, :] for hd in heads],
        [c_scr[hd] for hd in heads], [n_scr[hd:hd + 1, :] for hd in heads],
        [m_scr[hd:hd + 1, 0:1] for hd in heads])
    for hd in heads:
        c_scr[hd] = c_new[hd]
        n_scr[hd:hd + 1, :] = n_new[hd]
        m_scr[hd:hd + 1, :] = jnp.broadcast_to(m_new[hd], (1, LANES))
        hg_ref[:, v_cols[hd]] = _head_norm_gate(
            h[hd], o_ref[:, v_cols[hd]], gmh_ref[:, v_cols[hd]]).astype(BF16)

    @pl.when(c == pl.num_programs(1) - 1)
    def _():
        c_out[0] = c_scr[...]
        n_out[0] = n_scr[...]
        m_out[0] = m_scr[...]


def _cell_prompt(q, k, v, o, gates_t, g_mh, batch, side_weights):
    m_rows, hk = q.shape
    hv = v.shape[1]
    length = CELL_CHUNK
    nc = m_rows // batch // length
    rows = lambda width: pl.BlockSpec((length, width), lambda b, c: (b * nc + c, 0))
    side = _SideCasts(side_weights, batch * nc, lambda b, c: b * nc + c)
    outs = pl.pallas_call(
        functools.partial(_cell_prompt_kernel, len(side)),
        grid=(batch, nc),
        in_specs=[
            rows(hk), rows(hk), rows(hv), rows(hv),
            pl.BlockSpec((2 * N_HEADS, length), lambda b, c: (0, b * nc + c)),
            pl.BlockSpec((1, hv), lambda b, c: (0, 0)),
        ] + side.in_specs,
        out_specs=[
            rows(hv),
            pl.BlockSpec((1, N_HEADS, DQK_HEAD, DV_HEAD), lambda b, c: (b, 0, 0, 0)),
            pl.BlockSpec((1, N_HEADS, DQK_HEAD), lambda b, c: (b, 0, 0)),
            pl.BlockSpec((1, N_HEADS, LANES), lambda b, c: (b, 0, 0)),
        ] + side.out_specs,
        out_shape=[
            jax.ShapeDtypeStruct((m_rows, hv), BF16),
            jax.ShapeDtypeStruct((batch, N_HEADS, DQK_HEAD, DV_HEAD), F32),
            jax.ShapeDtypeStruct((batch, N_HEADS, DQK_HEAD), F32),
            jax.ShapeDtypeStruct((batch, N_HEADS, LANES), F32),
        ] + side.out_shapes,
        scratch_shapes=[
            pltpu.VMEM((N_HEADS, DQK_HEAD, DV_HEAD), F32),
            pltpu.VMEM((N_HEADS, DQK_HEAD), F32),
            pltpu.VMEM((N_HEADS, LANES), F32),
        ],
        compiler_params=_compiler_params(("parallel", "arbitrary")),
        name="cell_prompt",
    )(q, k, v, o, gates_t, g_mh, *side.args)
    return list(outs[:4]), list(outs[4:])


def _cell_sample_kernel(q_ref, k_ref, v_ref, o_ref, gt_ref, gmh_ref, c_in, n_in, m_in,
                        hg_ref, c_out, n_out, m_out):
    t_new = o_ref.shape[1]

    heads = range(N_HEADS)
    qk_cols = [slice(hd * DQK_HEAD, (hd + 1) * DQK_HEAD) for hd in heads]
    v_cols = [slice(hd * DV_HEAD, (hd + 1) * DV_HEAD) for hd in heads]

    def sequence_group(g, carry):
        probs = [(g * SAMPLE_UNROLL + u, hd) for u in range(SAMPLE_UNROLL) for hd in heads]
        h, c_new, n_new, m_new = _cell_chunks(
            [q_ref[b, :, qk_cols[hd]] for b, hd in probs], [k_ref[b, :, qk_cols[hd]] for b, hd in probs],
            [v_ref[b, :, v_cols[hd]] for b, hd in probs],
            [gt_ref[b, hd:hd + 1, :] for b, hd in probs],
            [gt_ref[b, N_HEADS + hd:N_HEADS + hd + 1, :] for b, hd in probs],
            [c_in[b, hd] for b, hd in probs], [n_in[b, hd:hd + 1, :] for b, hd in probs],
            [m_in[b, hd:hd + 1, 0:1] for b, hd in probs])
        for p, (b, hd) in enumerate(probs):
            c_out[b, hd] = c_new[p]
            n_out[b, hd:hd + 1, :] = n_new[p]
            m_out[b, hd:hd + 1, :] = jnp.broadcast_to(m_new[p], (1, LANES))
            hg_ref[b, :, v_cols[hd]] = _head_norm_gate(
                h[p][0:t_new, :], o_ref[b, :, v_cols[hd]], gmh_ref[:, v_cols[hd]])
        return carry

    lax.fori_loop(0, q_ref.shape[0] // SAMPLE_UNROLL, sequence_group, 0)


def _cell_sample(q, k, v, o, gates_t, g_mh, c0, n0, m0):
    nb, tp, hk = q.shape
    hv = v.shape[2]
    t_new = o.shape[1]
    nseq = SAMPLE_SEQS
    blk3 = lambda s1, s2: pl.BlockSpec((nseq, s1, s2), lambda i: (i, 0, 0))
    c_spec = pl.BlockSpec((nseq, N_HEADS, DQK_HEAD, DV_HEAD), lambda i: (i, 0, 0, 0))
    return pl.pallas_call(
        _cell_sample_kernel,
        grid=(nb // nseq,),
        in_specs=[
            blk3(tp, hk), blk3(tp, hk), blk3(tp, hv), blk3(t_new, hv),
            blk3(2 * N_HEADS, tp),
            pl.BlockSpec((1, hv), lambda i: (0, 0)),
            c_spec, blk3(N_HEADS, DQK_HEAD), blk3(N_HEADS, LANES),
        ],
        out_specs=[blk3(t_new, hv), c_spec, blk3(N_HEADS, DQK_HEAD), blk3(N_HEADS, LANES)],
        out_shape=[
            jax.ShapeDtypeStruct((nb, t_new, hv), F32),
            jax.ShapeDtypeStruct((nb, N_HEADS, DQK_HEAD, DV_HEAD), F32),
            jax.ShapeDtypeStruct((nb, N_HEADS, DQK_HEAD), F32),
            jax.ShapeDtypeStruct((nb, N_HEADS, LANES), F32),
        ],
        compiler_params=_compiler_params(("parallel",)),
        name="cell_sample",
    )(q, k, v, o, gates_t, g_mh, c0, n0, m0)


def _row(v):
    return v.reshape(1, -1).astype(F32)


def _swap_row_order(a, outer, inner):
    return jnp.swapaxes(a.reshape(outer, inner, -1), 0, 1).reshape(outer * inner, -1)


def _conv_layer(x, batch, past, p, side_weights):
    m_rows, d = x.shape
    u = _pw1_glu(x, p["g_pre"], p["w_pw1"], p["b_pw1"])
    side = []
    if past is None:
        y, side = _conv_prompt(u, batch, p["w_dw"], p["b_dw"], p["ln_g"], p["ln_b"], side_weights)
        state = u.reshape(batch, -1, d)[:, -CONV_STATE:]
    else:
        y_t, state_t = _conv_sample(
            jnp.swapaxes(past.astype(F32), 0, 1), u.reshape(m_rows // batch, batch, d),
            p["w_dw"], p["b_dw"], p["ln_g"], p["ln_b"])
        y = y_t.reshape(m_rows, d)
        state = jnp.swapaxes(state_t, 0, 1)
    return y, state, side


def _mlstm_layer(h, batch, state, p, side_weights):
    m_rows = h.shape[0]
    t_len = m_rows // batch
    q, k, v, o, gates = _qkvo(h, p["w_q"], p["w_k"], p["w_v"], p["w_o"], p["w_if"], p["b_if"])
    gates = gates[:, :2 * N_HEADS]
    side = []
    if state is None:
        (hg, c_new, n_new, m_new), side = _cell_prompt(
            q, k, v, o, gates.T, p["g_mh"], batch, side_weights)
    else:
        c0, n0, m0 = state
        pad_t = SAMPLE_PAD_T - t_len
        pad3 = lambda a: jnp.pad(a.reshape(batch, t_len, -1), ((0, 0), (0, pad_t), (0, 0)))
        g3 = gates.reshape(batch, t_len, 2 * N_HEADS)
        g_pad = jnp.concatenate(
            [jnp.full((batch, pad_t, N_HEADS), -jnp.inf, F32), jnp.zeros((batch, pad_t, N_HEADS), F32)], axis=2)
        gates_t = jnp.swapaxes(jnp.concatenate([g3, g_pad], axis=1), 1, 2)
        m0_lanes = jnp.broadcast_to(m0.astype(F32)[:, :, None], (batch, N_HEADS, LANES))
        hg, c_new, n_new, m_new = _cell_sample(
            pad3(q), pad3(k), pad3(v), o.reshape(batch, t_len, -1), gates_t, p["g_mh"],
            c0.astype(F32), n0.astype(F32), m0_lanes)
        hg = hg.reshape(m_rows, -1).astype(BF16)
    return hg, (c_new, n_new, m_new[:, :, 0]), side


def kernel(x_prompt, x_sample, cache_conv, state_C, state_n, state_m, conv_w_pw1, conv_b_pw1, conv_w_dw, conv_b_dw, conv_ln_g, conv_ln_b, conv_w_pw2, conv_b_pw2, mlstm_w_q, mlstm_w_k, mlstm_w_v, mlstm_w_o, mlstm_w_i, mlstm_b_i, mlstm_w_f, mlstm_b_f, mlstm_g_norm, mlstm_w_out, ffn_w_gate, ffn_w_up, ffn_w_down, norm_mix_pre, norm_mix_post, norm_ffn_pre, norm_ffn_post):
    depth = ffn_w_gate.shape[0]
    d = x_prompt.shape[-1]
    batches = (x_prompt.shape[0], x_sample.shape[0])
    xs = [x_prompt.reshape(-1, d), x_sample.reshape(-1, d)]
    hs = [None, None]
    conv_new = [[], []]
    c_new, n_new, m_new = [[], []], [[], []], [[], []]
    sample_batch, sample_t = x_sample.shape[0], x_sample.shape[1]
    sample_time_major = False
    next_own = None

    for i in range(depth):
        j = i // 2
        is_conv = i % 2 == 0
        if is_conv:
            p = {
                "g_pre": _row(norm_mix_pre[i]),
                "w_pw1": next_own[0] if next_own else conv_w_pw1[j].astype(BF16),
                "b_pw1": _row(conv_b_pw1[j]),
                "w_dw": conv_w_dw[j].astype(F32), "b_dw": _row(conv_b_dw[j]),
                "ln_g": _row(conv_ln_g[j]), "ln_b": _row(conv_ln_b[j]),
            }
            w_proj_all, b_proj = conv_w_pw2, _row(conv_b_pw2[j])
        else:
            w_if = jnp.concatenate([mlstm_w_i[j], mlstm_w_f[j]], axis=1)
            b_if = jnp.concatenate([mlstm_b_i[j], mlstm_b_f[j]], axis=0)
            pad = LANES - w_if.shape[1]
            own = next_own or [w[j].astype(BF16) for w in (mlstm_w_q, mlstm_w_k, mlstm_w_v, mlstm_w_o)]
            p = {
                "w_q": own[0], "w_k": own[1], "w_v": own[2], "w_o": own[3],
                "w_if": jnp.pad(w_if, ((0, 0), (0, pad))).astype(BF16),
                "b_if": jnp.pad(_row(b_if), ((0, 0), (0, pad))),
                "g_mh": _row(mlstm_g_norm[j]),
            }
            w_proj_all, b_proj = mlstm_w_out, None
        next_is_mlstm = i + 1 < depth and (i + 1) % 2 == 1
        g_next = _row(norm_mix_pre[i + 1]) if next_is_mlstm else None

        if sample_time_major != is_conv:
            dims = (sample_t, sample_batch) if sample_time_major else (sample_batch, sample_t)
            xs[1] = _swap_row_order(xs[1], *dims)
            hs[1] = None if hs[1] is None else _swap_row_order(hs[1], *dims)
            sample_time_major = is_conv

        to_cast = [(ffn_w_gate, i), (ffn_w_up, i), (ffn_w_down, i), (w_proj_all, j)]
        if i + 1 < depth:
            jn = (i + 1) // 2
            to_cast += ([(conv_w_pw1, jn)] if (i + 1) % 2 == 0 else
                        [(mlstm_w_q, jn), (mlstm_w_k, jn), (mlstm_w_v, jn), (mlstm_w_o, jn)])

        for gi in range(2):
            x, batch = xs[gi], batches[gi]
            side_weights = to_cast if gi == 0 else ()
            if is_conv:
                past = None if gi == 0 else cache_conv[j]
                mix_in, state, side = _conv_layer(x, batch, past, p, side_weights)
                conv_new[gi].append(state.astype(cache_conv.dtype))
            else:
                state = None if gi == 0 else (state_C[j], state_n[j], state_m[j])
                mix_in, (c1, n1, m1), side = _mlstm_layer(hs[gi], batch, state, p, side_weights)
                c_new[gi].append(c1.astype(state_C.dtype))
                n_new[gi].append(n1.astype(state_n.dtype))
                m_new[gi].append(m1.astype(state_m.dtype))
            if gi == 0:
                w_gate, w_up, w_down, w_proj = side[:4]
                next_own = side[4:]
            x, h_ffn = _proj_post(mix_in, w_proj, b_proj, x, _row(norm_mix_post[i]), _row(norm_ffn_pre[i]))
            xs[gi], hs[gi] = _ffn(h_ffn, w_gate, w_up, w_down, x, _row(norm_ffn_post[i]), g_next)

    if sample_time_major:
        xs[1] = _swap_row_order(xs[1], sample_t, sample_batch)

    return (xs[0].reshape(x_prompt.shape), xs[1].reshape(x_sample.shape),
            jnp.stack(conv_new[0]), jnp.stack(conv_new[1]),
            jnp.stack(c_new[0]), jnp.stack(n_new[0]), jnp.stack(m_new[0]),
            jnp.stack(c_new[1]), jnp.stack(n_new[1]), jnp.stack(m_new[1]))
```

```python
import functools

import jax
import jax.numpy as jnp
from jax import lax
from jax.experimental import pallas as pl
from jax.experimental.pallas import tpu as pltpu

F32 = jnp.float32
BF16 = jnp.bfloat16

D_MODEL = 2048
N_HEADS = 8
DV_HEAD = D_MODEL // N_HEADS
DQK_HEAD = DV_HEAD // 2
CONV_WIDTH = 31
CONV_STATE = CONV_WIDTH - 1
RMS_EPS = 1e-6
LN_EPS = 1e-5

LANES = 128
SUBLANES = 8
BF16_ROWS = 16
MIB = 1024 * 1024
VMEM_LIMIT = 52 * MIB

ROW_TILE = 1024
ROW_TILE_FULL_N = 512
PROJ_SUBTILES = 4
FFN_ROW_TILE = 1024
FFN_EPILOGUE_ROWS = 128
FFN_VMEM_LIMIT = 56 * MIB
CONV_TIME_TILE = 256
CONV_MIXER_COLS = 512
CONV_HALO = 32
CONV_ROW_STRIDE = 4
CELL_CHUNK = 256
SAMPLE_PAD_T = BF16_ROWS
SAMPLE_SEQS = 4
SAMPLE_UNROLL = 2
SAMPLE_CONV_SEQS = 16


def _compiler_params(semantics):
    return pltpu.CompilerParams(dimension_semantics=semantics, vmem_limit_bytes=VMEM_LIMIT)


def _dot(a, b):
    return jnp.dot(a, b, preferred_element_type=F32)


def _rms(x, g):
    return x * lax.rsqrt(jnp.mean(x * x, axis=-1, keepdims=True) + RMS_EPS) * g


def _sigmoid(x):
    return jax.nn.sigmoid(x)


class _SideCasts:
    def __init__(self, weights, steps, step_of):
        self.args, self.in_specs, self.out_specs, self.out_shapes = [], [], [], []
        for w, idx in weights:
            _, r, c = w.shape
            share = 1
            while (r * share) % steps or (r * share // steps) % BF16_ROWS:
                share *= 2
            rows = r * share // steps
            self.args.append(w)
            self.in_specs.append(pl.BlockSpec(
                (None, rows, c), lambda *g, idx=idx, share=share: (idx, step_of(*g) // share, 0)))
            self.out_specs.append(pl.BlockSpec(
                (rows, c), lambda *g, share=share: (step_of(*g) // share, 0)))
            self.out_shapes.append(jax.ShapeDtypeStruct((r, c), BF16))

    def __len__(self):
        return len(self.args)


def _run_side_casts(src_refs, dst_refs):
    for src, dst in zip(src_refs, dst_refs):
        dst[...] = src[...].astype(BF16)


def _pw1_glu_kernel(x_ref, g_ref, wa_ref, wg_ref, ba_ref, bg_ref, u_ref, h_scr):
    @pl.when(pl.program_id(1) == 0)
    def _():
        h_scr[...] = _rms(x_ref[...], g_ref[...]).astype(BF16)

    h = h_scr[...]
    a = _dot(h, wa_ref[...]) + ba_ref[...]
    g = _dot(h, wg_ref[...]) + bg_ref[...]
    u_ref[...] = a * _sigmoid(g)


def _pw1_glu(x, g_pre, w, b):
    m_rows, d = x.shape
    tm, tn = ROW_TILE, 512
    nn = d // tn
    return pl.pallas_call(
        _pw1_glu_kernel,
        grid=(m_rows // tm, nn),
        in_specs=[
            pl.BlockSpec((tm, d), lambda m, n: (m, 0)),
            pl.BlockSpec((1, d), lambda m, n: (0, 0)),
            pl.BlockSpec((d, tn), lambda m, n: (0, n)),
            pl.BlockSpec((d, tn), lambda m, n: (0, n + nn)),
            pl.BlockSpec((1, tn), lambda m, n: (0, n)),
            pl.BlockSpec((1, tn), lambda m, n: (0, n + nn)),
        ],
        out_specs=pl.BlockSpec((tm, tn), lambda m, n: (m, n)),
        out_shape=jax.ShapeDtypeStruct((m_rows, d), F32),
        scratch_shapes=[pltpu.VMEM((tm, d), BF16)],
        compiler_params=_compiler_params(("parallel", "arbitrary")),
        name="pw1_glu",
    )(x, g_pre, w, w, b, b)


def _ln_silu(v, ln_g, ln_b):
    xc = v - jnp.mean(v, axis=-1, keepdims=True)
    var = jnp.mean(xc * xc, axis=-1, keepdims=True)
    y = xc * lax.rsqrt(var + LN_EPS) * ln_g + ln_b
    return y * _sigmoid(y)


def _conv_taps(ext, cv, w_ref, b_ref, lt, tt):
    first_tap = CONV_HALO - CONV_STATE
    block_rows = SUBLANES * CONV_ROW_STRIDE
    lanes = slice(lt * LANES, (lt + 1) * LANES)
    bias = jnp.broadcast_to(b_ref[:, lanes], (SUBLANES, LANES))
    for r0 in range(0, tt, block_rows):
        accs = [bias] * CONV_ROW_STRIDE
        window = {}
        for j in range(CONV_WIDTH):
            wj = jnp.broadcast_to(w_ref[j:j + 1, lanes], (SUBLANES, LANES))
            for s in range(CONV_ROW_STRIDE):
                if s + j not in window:
                    window[s + j] = ext[lt, pl.ds(r0 + first_tap + s + j, SUBLANES, stride=CONV_ROW_STRIDE), :]
                accs[s] = accs[s] + wj * window[s + j]
        for s in range(CONV_ROW_STRIDE):
            cv[lt, pl.ds(r0 + s, SUBLANES, stride=CONV_ROW_STRIDE), :] = accs[s]


def _conv_mixer_prompt_kernel(n_side, tiles_per_seq, *refs):
    x_ref, g_ref, w_ref, bpw_ref, wdw_ref, bdw_ref, lg_ref, lb_ref = refs[:8]
    side_in = refs[8:8 + n_side]
    y_ref, tail_ref = refs[8 + n_side:10 + n_side]
    side_out = refs[10 + n_side:10 + 2 * n_side]
    ext, cv = refs[10 + 2 * n_side:]
    _run_side_casts(side_in, side_out)
    tt, d = x_ref.shape
    nl = d // LANES
    nn = d // CONV_MIXER_COLS
    tiles_per_block = CONV_MIXER_COLS // LANES
    t = pl.program_id(0) % tiles_per_seq

    @pl.when(t == 0)
    def _():
        ext[:, 0:CONV_HALO, :] = jnp.zeros((nl, CONV_HALO, LANES), F32)

    @pl.when(t != 0)
    def _():
        ext[:, 0:CONV_HALO, :] = ext[:, tt:tt + CONV_HALO, :]

    h = _rms(x_ref[...], g_ref[...]).astype(BF16)
    for n in range(nn + 1):
        if n < nn:
            a_cols = slice(n * CONV_MIXER_COLS, (n + 1) * CONV_MIXER_COLS)
            g_cols = slice(d + n * CONV_MIXER_COLS, d + (n + 1) * CONV_MIXER_COLS)
            a = _dot(h, w_ref[:, a_cols]) + bpw_ref[:, a_cols]
            g = _dot(h, w_ref[:, g_cols]) + bpw_ref[:, g_cols]
            u = a * _sigmoid(g)
            for l in range(tiles_per_block):
                ext[n * tiles_per_block + l, CONV_HALO:CONV_HALO + tt, :] = u[:, l * LANES:(l + 1) * LANES]
        if n > 0:
            for l in range(tiles_per_block):
                _conv_taps(ext, cv, wdw_ref, bdw_ref, (n - 1) * tiles_per_block + l, tt)

    for lt in range(nl):
        tail_ref[:, lt * LANES:(lt + 1) * LANES] = ext[lt, tt:tt + CONV_HALO, :]
    v = cv[...]
    xc = v - jnp.sum(jnp.sum(v, axis=0), axis=-1, keepdims=True) / d
    var = jnp.sum(jnp.sum(xc * xc, axis=0), axis=-1, keepdims=True) / d
    inv = lax.rsqrt(var + LN_EPS)
    for lt in range(nl):
        cols = slice(lt * LANES, (lt + 1) * LANES)
        y = xc[lt] * inv * lg_ref[:, cols] + lb_ref[:, cols]
        y_ref[:, cols] = (y * _sigmoid(y)).astype(BF16)


def _conv_mixer_prompt(x, batch, g_pre, w_pw1, b_pw1, w_dw, b_dw, ln_g, ln_b, side_weights):
    m_rows, d = x.shape
    tt = CONV_TIME_TILE
    nt = m_rows // batch // tt
    vec = pl.BlockSpec((1, d), lambda i: (0, 0))
    side = _SideCasts(side_weights, m_rows // tt, lambda i: i)
    outs = pl.pallas_call(
        functools.partial(_conv_mixer_prompt_kernel, len(side), nt),
        grid=(m_rows // tt,),
        in_specs=[
            pl.BlockSpec((tt, d), lambda i: (i, 0)), vec,
            pl.BlockSpec((d, 2 * d), lambda i: (0, 0), pipeline_mode=pl.Buffered(1)),
            pl.BlockSpec((1, 2 * d), lambda i: (0, 0)),
            pl.BlockSpec((CONV_WIDTH, d), lambda i: (0, 0)),
            vec, vec, vec,
        ] + side.in_specs,
        out_specs=[pl.BlockSpec((tt, d), lambda i: (i, 0)),
                   pl.BlockSpec((CONV_HALO, d), lambda i: (i, 0))] + side.out_specs,
        out_shape=[jax.ShapeDtypeStruct((m_rows, d), BF16),
                   jax.ShapeDtypeStruct((m_rows // tt * CONV_HALO, d), F32)] + side.out_shapes,
        scratch_shapes=[pltpu.VMEM((d // LANES, CONV_HALO + tt, LANES), F32),
                        pltpu.VMEM((d // LANES, tt, LANES), F32)],
        compiler_params=_compiler_params(("arbitrary",)),
        name="conv_mixer_prompt",
    )(x, g_pre, w_pw1, b_pw1, w_dw, b_dw, ln_g, ln_b, *side.args)
    tails = outs[1].reshape(batch, nt, CONV_HALO, d)
    return outs[0], tails[:, -1, CONV_HALO - CONV_STATE:], list(outs[2:])


def _conv_sample_kernel(cache_ref, u_ref, w_ref, b_ref, lg_ref, lb_ref, y_ref, state_ref, cv):
    hist, nseq, d = cache_ref.shape
    t_new = u_ref.shape[0]
    state_ref[0:hist - t_new] = cache_ref[t_new:hist]
    state_ref[hist - t_new:hist] = u_ref[...]

    def history_row(r, seqs, lanes):
        if r < hist:
            return cache_ref[r, seqs, lanes]
        return u_ref[r - hist, seqs, lanes]

    def lane_tile(lt, carry):
        lanes = pl.ds(pl.multiple_of(lt * LANES, LANES), LANES)
        bias = jnp.broadcast_to(b_ref[:, lanes], (SUBLANES, LANES))
        for s0 in range(0, nseq, SUBLANES):
            seqs = slice(s0, s0 + SUBLANES)
            accs = [bias] * t_new
            for j in range(CONV_WIDTH):
                wj = jnp.broadcast_to(w_ref[j:j + 1, lanes], (SUBLANES, LANES))
                for t in range(t_new):
                    accs[t] = accs[t] + wj * history_row(t + j, seqs, lanes)
            for t in range(t_new):
                cv[t, seqs, lanes] = accs[t]
        return carry

    lax.fori_loop(0, d // LANES, lane_tile, 0)
    y_ref[...] = _ln_silu(cv[...], lg_ref[...], lb_ref[...]).astype(BF16)


def _conv_sample(cache_t, u_t, w_dw, b_dw, ln_g, ln_b):
    hist, nb, d = cache_t.shape
    t_new = u_t.shape[0]
    nseq = SAMPLE_CONV_SEQS
    vec = pl.BlockSpec((1, d), lambda i: (0, 0))
    seq_block = lambda rows: pl.BlockSpec((rows, nseq, d), lambda i: (0, i, 0))
    return pl.pallas_call(
        _conv_sample_kernel,
        grid=(nb // nseq,),
        in_specs=[
            seq_block(hist), seq_block(t_new),
            pl.BlockSpec((CONV_WIDTH, d), lambda i: (0, 0)),
            vec, vec, vec,
        ],
        out_specs=[seq_block(t_new), seq_block(hist)],
        out_shape=[jax.ShapeDtypeStruct((t_new, nb, d), BF16),
                   jax.ShapeDtypeStruct((hist, nb, d), F32)],
        scratch_shapes=[pltpu.VMEM((t_new, nseq, d), F32)],
        compiler_params=_compiler_params(("parallel",)),
        name="conv_sample",
    )(cache_t, u_t, w_dw, b_dw, ln_g, ln_b)


def _proj_post_kernel(has_bias, *refs):
    if has_bias:
        a_ref, w_ref, b_ref, x_ref, gpost_ref, gnext_ref, xo_ref, ho_ref = refs
    else:
        a_ref, w_ref, x_ref, gpost_ref, gnext_ref, xo_ref, ho_ref = refs
    w = w_ref[...]
    sub = a_ref.shape[0] // PROJ_SUBTILES

    def finish(i, mix):
        rows = slice(i * sub, (i + 1) * sub)
        if has_bias:
            mix = mix + b_ref[...]
        x1 = x_ref[rows, :] + _rms(mix, gpost_ref[...])
        xo_ref[rows, :] = x1
        ho_ref[rows, :] = _rms(x1, gnext_ref[...]).astype(BF16)

    pending = None
    for i in range(PROJ_SUBTILES):
        mix = _dot(a_ref[i * sub:(i + 1) * sub, :], w)
        if pending is not None:
            finish(i - 1, pending)
        pending = mix
    finish(PROJ_SUBTILES - 1, pending)


def _proj_post(a, w, b, x, g_post, g_next):
    m_rows, k = a.shape
    d = w.shape[1]
    tm = ROW_TILE_FULL_N
    row = lambda width: pl.BlockSpec((tm, width), lambda m: (m, 0))
    vec = pl.BlockSpec((1, d), lambda m: (0, 0))
    has_bias = b is not None
    in_specs = [row(k), pl.BlockSpec((k, d), lambda m: (0, 0))]
    args = [a, w]
    if has_bias:
        in_specs.append(vec)
        args.append(b)
    in_specs += [row(d), vec, vec]
    args += [x, g_post, g_next]
    return pl.pallas_call(
        functools.partial(_proj_post_kernel, has_bias),
        grid=(m_rows // tm,),
        in_specs=in_specs,
        out_specs=[row(d), row(d)],
        out_shape=[jax.ShapeDtypeStruct((m_rows, d), F32), jax.ShapeDtypeStruct((m_rows, d), BF16)],
        compiler_params=_compiler_params(("parallel",)),
        name="proj_post",
    )(*args)


def _ffn_kernel(has_next, *refs):
    if has_next:
        h_ref, wg_ref, wu_ref, wd_ref, x_hbm, gpost_ref, gnext_ref, xo_ref, ho_ref, x_buf, x_sem = refs
    else:
        h_ref, wg_ref, wu_ref, wd_ref, x_hbm, gpost_ref, xo_ref, x_buf, x_sem = refs
    f = pl.program_id(1)
    tm = xo_ref.shape[0]
    row0 = pl.multiple_of(pl.program_id(0) * tm, tm)
    x_copy = pltpu.make_async_copy(x_hbm.at[pl.ds(row0, tm), :], x_buf, x_sem)

    @pl.when(f == 0)
    def _():
        x_copy.start()
        xo_ref[...] = jnp.zeros(xo_ref.shape, F32)

    h = h_ref[...]
    gate = _dot(h, wg_ref[...])
    up = _dot(h, wu_ref[...])
    xo_ref[...] += _dot((gate * _sigmoid(gate) * up).astype(BF16), wd_ref[...])

    @pl.when(f == pl.num_programs(1) - 1)
    def _():
        x_copy.wait()
        for r0 in range(0, tm, FFN_EPILOGUE_ROWS):
            rows = slice(r0, r0 + FFN_EPILOGUE_ROWS)
            x2 = x_buf[rows, :] + _rms(xo_ref[rows, :], gpost_ref[...])
            xo_ref[rows, :] = x2
            if has_next:
                ho_ref[rows, :] = _rms(x2, gnext_ref[...]).astype(BF16)


def _ffn(h, w_gate, w_up, w_down, x, g_post, g_next):
    m_rows, d = h.shape
    d_ff = w_gate.shape[1]
    tm, tf = FFN_ROW_TILE, 512
    row = pl.BlockSpec((tm, d), lambda m, f: (m, 0))
    vec = pl.BlockSpec((1, d), lambda m, f: (0, 0))
    has_next = g_next is not None
    in_specs = [
        pl.BlockSpec((tm, d), lambda m, f: (m, 0), pipeline_mode=pl.Buffered(1)),
        pl.BlockSpec((d, tf), lambda m, f: (0, f)),
        pl.BlockSpec((d, tf), lambda m, f: (0, f)),
        pl.BlockSpec((tf, d), lambda m, f: (f, 0)),
        pl.BlockSpec(memory_space=pl.ANY), vec,
    ]
    args = [h, w_gate, w_up, w_down, x, g_post]
    out_specs = [row]
    out_shape = [jax.ShapeDtypeStruct((m_rows, d), F32)]
    if has_next:
        in_specs.append(vec)
        args.append(g_next)
        out_specs.append(row)
        out_shape.append(jax.ShapeDtypeStruct((m_rows, d), BF16))
    outs = pl.pallas_call(
        functools.partial(_ffn_kernel, has_next),
        grid=(m_rows // tm, d_ff // tf),
        in_specs=in_specs,
        out_specs=out_specs,
        out_shape=out_shape,
        scratch_shapes=[pltpu.VMEM((tm, d), F32), pltpu.SemaphoreType.DMA(())],
        compiler_params=pltpu.CompilerParams(
            dimension_semantics=("parallel", "arbitrary"), vmem_limit_bytes=FFN_VMEM_LIMIT),
        name="ffn",
    )(*args)
    return (outs[0], outs[1]) if has_next else (outs[0], None)


def _qkvo_kernel(h_ref, wq_ref, wk_ref, wv_ref, wo_ref, wif_ref, bif_ref,
                 q_ref, k_ref, v_ref, o_ref, g_ref):
    h = h_ref[...]
    q_ref[...] = (_dot(h, wq_ref[...]) * (DQK_HEAD ** -0.5)).astype(BF16)
    k_ref[...] = _dot(h, wk_ref[...]).astype(BF16)
    v_ref[...] = _dot(h, wv_ref[...]).astype(BF16)
    o_ref[...] = _sigmoid(_dot(h, wo_ref[...]))

    @pl.when(pl.program_id(1) == 0)
    def _():
        z = _dot(h, wif_ref[...]) + bif_ref[...]
        lane = lax.broadcasted_iota(jnp.int32, z.shape, 1)
        log_sig = jnp.minimum(z, 0.0) - jnp.log1p(jnp.exp(-jnp.abs(z)))
        g_ref[...] = jnp.where(lane < N_HEADS, z, log_sig)


def _qkvo(h, w_q, w_k, w_v, w_o, w_if, b_if):
    m_rows, d = h.shape
    hk, hv = w_q.shape[1], w_v.shape[1]
    tm, steps = ROW_TILE, 4
    col = lambda width: pl.BlockSpec((d, width), lambda m, n: (0, n))
    out = lambda width: pl.BlockSpec((tm, width), lambda m, n: (m, n))
    return pl.pallas_call(
        _qkvo_kernel,
        grid=(m_rows // tm, steps),
        in_specs=[
            pl.BlockSpec((tm, d), lambda m, n: (m, 0)),
            col(hk // steps), col(hk // steps), col(hv // steps), col(hv // steps),
            pl.BlockSpec((d, LANES), lambda m, n: (0, 0)),
            pl.BlockSpec((1, LANES), lambda m, n: (0, 0)),
        ],
        out_specs=[out(hk // steps), out(hk // steps), out(hv // steps), out(hv // steps),
                   pl.BlockSpec((tm, LANES), lambda m, n: (m, 0))],
        out_shape=[
            jax.ShapeDtypeStruct((m_rows, hk), BF16),
            jax.ShapeDtypeStruct((m_rows, hk), BF16),
            jax.ShapeDtypeStruct((m_rows, hv), BF16),
            jax.ShapeDtypeStruct((m_rows, hv), F32),
            jax.ShapeDtypeStruct((m_rows, LANES), F32),
        ],
        compiler_params=_compiler_params(("parallel", "arbitrary")),
        name="qkvo",
    )(h, w_q, w_k, w_v, w_o, w_if, b_if)


def _cell_chunks(q, k, v, log_i, log_f, c_state, n_state, m_state):
    probs = range(len(q))
    length = q[0].shape[0]
    t_idx = lax.broadcasted_iota(jnp.int32, (length, length), 0)
    s_idx = lax.broadcasted_iota(jnp.int32, (length, length), 1)
    causal = s_idx <= t_idx
    eye = s_idx == t_idx
    nt_dims = (((1,), (1,)), ((), ()))
    tn_dims = (((0,), (0,)), ((), ()))

    qk = [lax.dot_general(q[p], k[p], nt_dims, preferred_element_type=F32) for p in probs]
    inter = [_dot(q[p], c_state[p].astype(BF16)) for p in probs]
    b_col = [jnp.sum(jnp.where(causal, log_f[p], 0.0), axis=1, keepdims=True) for p in probs]
    b_row = [jnp.sum(jnp.where(eye, b_col[p], 0.0), axis=0, keepdims=True) for p in probs]
    i_col = [jnp.sum(jnp.where(eye, log_i[p], 0.0), axis=1, keepdims=True) for p in probs]
    a_inter = [b_col[p] + m_state[p] for p in probs]
    d_intra = [jnp.where(causal, b_col[p] - b_row[p] + log_i[p], -jnp.inf) for p in probs]
    m_t = [jnp.maximum(a_inter[p], jnp.max(d_intra[p], axis=1, keepdims=True)) for p in probs]
    w_inter = [jnp.exp(a_inter[p] - m_t[p]) for p in probs]
    s = [qk[p] * jnp.exp(d_intra[p] - m_t[p]) for p in probs]
    intra = [_dot(s[p].astype(BF16), v[p]) for p in probs]

    def den(p):
        qn = w_inter[p] * (q[p].astype(F32) * n_state[p])
        if length % LANES == 0:
            for c0 in range(0, length, LANES):
                qn = qn + s[p][:, c0:c0 + LANES]
            return jnp.sum(qn, axis=1, keepdims=True)
        return jnp.sum(qn, axis=1, keepdims=True) + jnp.sum(s[p], axis=1, keepdims=True)

    h = [(w_inter[p] * inter[p] + intra[p]) / jnp.maximum(jnp.abs(den(p)), jnp.exp(-m_t[p]))
         for p in probs]

    m_new = [m_t[p][length - 1:length, :] for p in probs]
    b_last = [b_col[p][length - 1:length, :] for p in probs]
    g_state = [jnp.exp(b_last[p] + m_state[p] - m_new[p]) for p in probs]
    kw = [k[p].astype(F32) * jnp.exp(b_last[p] - b_col[p] + i_col[p] - m_new[p]) for p in probs]
    c_new = [g_state[p] * c_state[p]
             + lax.dot_general(kw[p].astype(BF16), v[p], tn_dims, preferred_element_type=F32)
             for p in probs]
    n_new = [g_state[p] * n_state[p] + jnp.sum(kw[p], axis=0, keepdims=True) for p in probs]
    return h, c_new, n_new, m_new


def _head_norm_gate(h, o, g_mh):
    hn = h * lax.rsqrt(jnp.mean(h * h, axis=-1, keepdims=True) + RMS_EPS)
    return o * (hn * g_mh)


def _cell_prompt_kernel(n_side, *refs):
    q_ref, k_ref, v_ref, o_ref, gt_ref, gmh_ref = refs[:6]
    side_in = refs[6:6 + n_side]
    hg_ref, c_out, n_out, m_out = refs[6 + n_side:10 + n_side]
    side_out = refs[10 + n_side:10 + 2 * n_side]
    c_scr, n_scr, m_scr = refs[10 + 2 * n_side:]
    _run_side_casts(side_in, side_out)
    c = pl.program_id(1)

    @pl.when(c == 0)
    def _():
        c_scr[...] = jnp.zeros(c_scr.shape, F32)
        n_scr[...] = jnp.zeros(n_scr.shape, F32)
        m_scr[...] = jnp.zeros(m_scr.shape, F32)

    heads = range(N_HEADS)
    qk_cols = [slice(hd * DQK_HEAD, (hd + 1) * DQK_HEAD) for hd in heads]
    v_cols = [slice(hd * DV_HEAD, (hd + 1) * DV_HEAD) for hd in heads]
    h, c_new, n_new, m_new = _cell_chunks(
        [q_ref[:, qk_cols[hd]] for hd in heads], [k_ref[:, qk_cols[hd]] for hd in heads],
        [v_ref[:, v_cols[hd]] for hd in heads],
        [gt_ref[hd:hd + 1, :] for hd in heads],
        [gt_ref[N_HEADS + hd:N_HEADS + hd + 1, :] for hd in heads],
        [c_scr[hd] for hd in heads], [n_scr[hd:hd + 1, :] for hd in heads],
        [m_scr[hd:hd + 1, 0:1] for hd in heads])
    for hd in heads:
        c_scr[hd] = c_new[hd]
        n_scr[hd:hd + 1, :] = n_new[hd]
        m_scr[hd:hd + 1, :] = jnp.broadcast_to(m_new[hd], (1, LANES))
        hg_ref[:, v_cols[hd]] = _head_norm_gate(
            h[hd], o_ref[:, v_cols[hd]], gmh_ref[:, v_cols[hd]]).astype(BF16)

    @pl.when(c == pl.num_programs(1) - 1)
    def _():
        c_out[0] = c_scr[...]
        n_out[0] = n_scr[...]
        m_out[0] = m_scr[...]


def _cell_prompt(q, k, v, o, gates_t, g_mh, batch, side_weights):
    m_rows, hk = q.shape
    hv = v.shape[1]
    length = CELL_CHUNK
    nc = m_rows // batch // length
    rows = lambda width: pl.BlockSpec((length, width), lambda b, c: (b * nc + c, 0))
    side = _SideCasts(side_weights, batch * nc, lambda b, c: b * nc + c)
    outs = pl.pallas_call(
        functools.partial(_cell_prompt_kernel, len(side)),
        grid=(batch, nc),
        in_specs=[
            rows(hk), rows(hk), rows(hv), rows(hv),
            pl.BlockSpec((2 * N_HEADS, length), lambda b, c: (0, b * nc + c)),
            pl.BlockSpec((1, hv), lambda b, c: (0, 0)),
        ] + side.in_specs,
        out_specs=[
            rows(hv),
            pl.BlockSpec((1, N_HEADS, DQK_HEAD, DV_HEAD), lambda b, c: (b, 0, 0, 0)),
            pl.BlockSpec((1, N_HEADS, DQK_HEAD), lambda b, c: (b, 0, 0)),
            pl.BlockSpec((1, N_HEADS, LANES), lambda b, c: (b, 0, 0)),
        ] + side.out_specs,
        out_shape=[
            jax.ShapeDtypeStruct((m_rows, hv), BF16),
            jax.ShapeDtypeStruct((batch, N_HEADS, DQK_HEAD, DV_HEAD), F32),
            jax.ShapeDtypeStruct((batch, N_HEADS, DQK_HEAD), F32),
            jax.ShapeDtypeStruct((batch, N_HEADS, LANES), F32),
        ] + side.out_shapes,
        scratch_shapes=[
            pltpu.VMEM((N_HEADS, DQK_HEAD, DV_HEAD), F32),
            pltpu.VMEM((N_HEADS, DQK_HEAD), F32),
            pltpu.VMEM((N_HEADS, LANES), F32),
        ],
        compiler_params=_compiler_params(("parallel", "arbitrary")),
        name="cell_prompt",
    )(q, k, v, o, gates_t, g_mh, *side.args)
    return list(outs[:4]), list(outs[4:])


def _cell_sample_kernel(q_ref, k_ref, v_ref, o_ref, gt_ref, gmh_ref, c_in, n_in, m_in,
                        hg_ref, c_out, n_out, m_out):
    t_new = o_ref.shape[1]

    heads = range(N_HEADS)
    qk_cols = [slice(hd * DQK_HEAD, (hd + 1) * DQK_HEAD) for hd in heads]
    v_cols = [slice(hd * DV_HEAD, (hd + 1) * DV_HEAD) for hd in heads]

    def sequence_group(g, carry):
        probs = [(g * SAMPLE_UNROLL + u, hd) for u in range(SAMPLE_UNROLL) for hd in heads]
        h, c_new, n_new, m_new = _cell_chunks(
            [q_ref[b, :, qk_cols[hd]] for b, hd in probs], [k_ref[b, :, qk_cols[hd]] for b, hd in probs],
            [v_ref[b, :, v_cols[hd]] for b, hd in probs],
            [gt_ref[b, hd:hd + 1, :] for b, hd in probs],
            [gt_ref[b, N_HEADS + hd:N_HEADS + hd + 1, :] for b, hd in probs],
            [c_in[b, hd] for b, hd in probs], [n_in[b, hd:hd + 1, :] for b, hd in probs],
            [m_in[b, hd:hd + 1, 0:1] for b, hd in probs])
        for p, (b, hd) in enumerate(probs):
            c_out[b, hd] = c_new[p]
            n_out[b, hd:hd + 1, :] = n_new[p]
            m_out[b, hd:hd + 1, :] = jnp.broadcast_to(m_new[p], (1, LANES))
            hg_ref[b, :, v_cols[hd]] = _head_norm_gate(
                h[p][0:t_new, :], o_ref[b, :, v_cols[hd]], gmh_ref[:, v_cols[hd]])
        return carry

    lax.fori_loop(0, q_ref.shape[0] // SAMPLE_UNROLL, sequence_group, 0)


def _cell_sample(q, k, v, o, gates_t, g_mh, c0, n0, m0):
    nb, tp, hk = q.shape
    hv = v.shape[2]
    t_new = o.shape[1]
    nseq = SAMPLE_SEQS
    blk3 = lambda s1, s2: pl.BlockSpec((nseq, s1, s2), lambda i: (i, 0, 0))
    c_spec = pl.BlockSpec((nseq, N_HEADS, DQK_HEAD, DV_HEAD), lambda i: (i, 0, 0, 0))
    return pl.pallas_call(
        _cell_sample_kernel,
        grid=(nb // nseq,),
        in_specs=[
            blk3(tp, hk), blk3(tp, hk), blk3(tp, hv), blk3(t_new, hv),
            blk3(2 * N_HEADS, tp),
            pl.BlockSpec((1, hv), lambda i: (0, 0)),
            c_spec, blk3(N_HEADS, DQK_HEAD), blk3(N_HEADS, LANES),
        ],
        out_specs=[blk3(t_new, hv), c_spec, blk3(N_HEADS, DQK_HEAD), blk3(N_HEADS, LANES)],
        out_shape=[
            jax.ShapeDtypeStruct((nb, t_new, hv), F32),
            jax.ShapeDtypeStruct((nb, N_HEADS, DQK_HEAD, DV_HEAD), F32),
            jax.ShapeDtypeStruct((nb, N_HEADS, DQK_HEAD), F32),
            jax.ShapeDtypeStruct((nb, N_HEADS, LANES), F32),
        ],
        compiler_params=_compiler_params(("parallel",)),
        name="cell_sample",
    )(q, k, v, o, gates_t, g_mh, c0, n0, m0)


def _row(v):
    return v.reshape(1, -1).astype(F32)


def _swap_row_order(a, outer, inner):
    return jnp.swapaxes(a.reshape(outer, inner, -1), 0, 1).reshape(outer * inner, -1)


def _conv_layer(x, batch, past, p, side_weights):
    m_rows, d = x.shape
    if past is None:
        y, state, side = _conv_mixer_prompt(
            x, batch, p["g_pre"], p["w_pw1"], p["b_pw1"], p["w_dw"], p["b_dw"], p["ln_g"], p["ln_b"],
            side_weights)
    else:
        side = []
        u = _pw1_glu(x, p["g_pre"], p["w_pw1"], p["b_pw1"])
        y_t, state_t = _conv_sample(
            jnp.swapaxes(past.astype(F32), 0, 1), u.reshape(m_rows // batch, batch, d),
            p["w_dw"], p["b_dw"], p["ln_g"], p["ln_b"])
        y = y_t.reshape(m_rows, d)
        state = jnp.swapaxes(state_t, 0, 1)
    return y, state, side


def _mlstm_layer(h, batch, state, p, side_weights):
    m_rows = h.shape[0]
    t_len = m_rows // batch
    q, k, v, o, gates = _qkvo(h, p["w_q"], p["w_k"], p["w_v"], p["w_o"], p["w_if"], p["b_if"])
    gates = gates[:, :2 * N_HEADS]
    side = []
    if state is None:
        (hg, c_new, n_new, m_new), side = _cell_prompt(
            q, k, v, o, gates.T, p["g_mh"], batch, side_weights)
    else:
        c0, n0, m0 = state
        pad_t = SAMPLE_PAD_T - t_len
        pad3 = lambda a: jnp.pad(a.reshape(batch, t_len, -1), ((0, 0), (0, pad_t), (0, 0)))
        g3 = gates.reshape(batch, t_len, 2 * N_HEADS)
        g_pad = jnp.concatenate(
            [jnp.full((batch, pad_t, N_HEADS), -jnp.inf, F32), jnp.zeros((batch, pad_t, N_HEADS), F32)], axis=2)
        gates_t = jnp.swapaxes(jnp.concatenate([g3, g_pad], axis=1), 1, 2)
        m0_lanes = jnp.broadcast_to(m0.astype(F32)[:, :, None], (batch, N_HEADS, LANES))
        hg, c_new, n_new, m_new = _cell_sample(
            pad3(q), pad3(k), pad3(v), o.reshape(batch, t_len, -1), gates_t, p["g_mh"],
            c0.astype(F32), n0.astype(F32), m0_lanes)
        hg = hg.reshape(m_rows, -1).astype(BF16)
    return hg, (c_new, n_new, m_new[:, :, 0]), side


def kernel(x_prompt, x_sample, cache_conv, state_C, state_n, state_m, conv_w_pw1, conv_b_pw1, conv_w_dw, conv_b_dw, conv_ln_g, conv_ln_b, conv_w_pw2, conv_b_pw2, mlstm_w_q, mlstm_w_k, mlstm_w_v, mlstm_w_o, mlstm_w_i, mlstm_b_i, mlstm_w_f, mlstm_b_f, mlstm_g_norm, mlstm_w_out, ffn_w_gate, ffn_w_up, ffn_w_down, norm_mix_pre, norm_mix_post, norm_ffn_pre, norm_ffn_post):
    depth = ffn_w_gate.shape[0]
    d = x_prompt.shape[-1]
    batches = (x_prompt.shape[0], x_sample.shape[0])
    xs = [x_prompt.reshape(-1, d), x_sample.reshape(-1, d)]
    hs = [None, None]
    conv_new = [[], []]
    c_new, n_new, m_new = [[], []], [[], []], [[], []]
    sample_batch, sample_t = x_sample.shape[0], x_sample.shape[1]
    sample_time_major = False
    next_own = None

    for i in range(depth):
        j = i // 2
        is_conv = i % 2 == 0
        if is_conv:
            p = {
                "g_pre": _row(norm_mix_pre[i]),
                "w_pw1": next_own[0] if next_own else conv_w_pw1[j].astype(BF16),
                "b_pw1": _row(conv_b_pw1[j]),
                "w_dw": conv_w_dw[j].astype(F32), "b_dw": _row(conv_b_dw[j]),
                "ln_g": _row(conv_ln_g[j]), "ln_b": _row(conv_ln_b[j]),
            }
            w_proj_all, b_proj = conv_w_pw2, _row(conv_b_pw2[j])
        else:
            w_if = jnp.concatenate([mlstm_w_i[j], mlstm_w_f[j]], axis=1)
            b_if = jnp.concatenate([mlstm_b_i[j], mlstm_b_f[j]], axis=0)
            pad = LANES - w_if.shape[1]
            own = next_own or [w[j].astype(BF16) for w in (mlstm_w_q, mlstm_w_k, mlstm_w_v, mlstm_w_o)]
            p = {
                "w_q": own[0], "w_k": own[1], "w_v": own[2], "w_o": own[3],
                "w_if": jnp.pad(w_if, ((0, 0), (0, pad))).astype(BF16),
                "b_if": jnp.pad(_row(b_if), ((0, 0), (0, pad))),
                "g_mh": _row(mlstm_g_norm[j]),
            }
            w_proj_all, b_proj = mlstm_w_out, None
        next_is_mlstm = i + 1 < depth and (i + 1) % 2 == 1
        g_next = _row(norm_mix_pre[i + 1]) if next_is_mlstm else None

        if sample_time_major != is_conv:
            dims = (sample_t, sample_batch) if sample_time_major else (sample_batch, sample_t)
            xs[1] = _swap_row_order(xs[1], *dims)
            hs[1] = None if hs[1] is None else _swap_row_order(hs[1], *dims)
            sample_time_major = is_conv

        to_cast = [(ffn_w_gate, i), (ffn_w_up, i), (ffn_w_down, i), (w_proj_all, j)]
        if i + 1 < depth:
            jn = (i + 1) // 2
            to_cast += ([(conv_w_pw1, jn)] if (i + 1) % 2 == 0 else
                        [(mlstm_w_q, jn), (mlstm_w_k, jn), (mlstm_w_v, jn), (mlstm_w_o, jn)])

        for gi in range(2):
            x, batch = xs[gi], batches[gi]
            side_weights = to_cast if gi == 0 else ()
            if is_conv:
                past = None if gi == 0 else cache_conv[j]
                mix_in, state, side = _conv_layer(x, batch, past, p, side_weights)
                conv_new[gi].append(state.astype(cache_conv.dtype))
            else:
                state = None if gi == 0 else (state_C[j], state_n[j], state_m[j])
                mix_in, (c1, n1, m1), side = _mlstm_layer(hs[gi], batch, state, p, side_weights)
                c_new[gi].append(c1.astype(state_C.dtype))
                n_new[gi].append(n1.astype(state_n.dtype))
                m_new[gi].append(m1.astype(state_m.dtype))
            if gi == 0:
                w_gate, w_up, w_down, w_proj = side[:4]
                next_own = side[4:]
            x, h_ffn = _proj_post(mix_in, w_proj, b_proj, x, _row(norm_mix_post[i]), _row(norm_ffn_pre[i]))
            xs[gi], hs[gi] = _ffn(h_ffn, w_gate, w_up, w_down, x, _row(norm_ffn_post[i]), g_next)

    if sample_time_major:
        xs[1] = _swap_row_order(xs[1], sample_t, sample_batch)

    return (xs[0].reshape(x_prompt.shape), xs[1].reshape(x_sample.shape),
            jnp.stack(conv_new[0]), jnp.stack(conv_new[1]),
            jnp.stack(c_new[0]), jnp.stack(n_new[0]), jnp.stack(m_new[0]),
            jnp.stack(c_new[1]), jnp.stack(n_new[1]), jnp.stack(m_new[1]))
```

```python
import functools

import jax
import jax.numpy as jnp
from jax import lax
from jax.experimental import pallas as pl
from jax.experimental.pallas import tpu as pltpu

F32 = jnp.float32
BF16 = jnp.bfloat16

D_MODEL = 2048
N_HEADS = 8
DV_HEAD = D_MODEL // N_HEADS
DQK_HEAD = DV_HEAD // 2
CONV_WIDTH = 31
CONV_STATE = CONV_WIDTH - 1
RMS_EPS = 1e-6
LN_EPS = 1e-5

LANES = 128
SUBLANES = 8
BF16_ROWS = 16
MIB = 1024 * 1024
VMEM_LIMIT = 52 * MIB

ROW_TILE = 1024
ROW_TILE_FULL_N = 512
PROJ_SUBTILES = 4
FFN_ROW_TILE = 1024
FFN_EPILOGUE_ROWS = 128
FFN_VMEM_LIMIT = 56 * MIB
CONV_TIME_TILE = 256
CONV_MIXER_COLS = 256
CONV_HALO = 32
CONV_ROW_STRIDE = 4
CELL_CHUNK = 256
SAMPLE_PAD_T = BF16_ROWS
SAMPLE_SEQS = 8
SAMPLE_UNROLL = 2
SAMPLE_CONV_SEQS = 16


def _compiler_params(semantics):
    return pltpu.CompilerParams(dimension_semantics=semantics, vmem_limit_bytes=VMEM_LIMIT)


def _dot(a, b):
    return jnp.dot(a, b, preferred_element_type=F32)


def _rms(x, g):
    return x * lax.rsqrt(jnp.mean(x * x, axis=-1, keepdims=True) + RMS_EPS) * g


def _sigmoid(x):
    return jax.nn.sigmoid(x)


class _SideCasts:
    def __init__(self, weights, steps, step_of):
        self.args, self.in_specs, self.out_specs, self.out_shapes = [], [], [], []
        for w, idx in weights:
            _, r, c = w.shape
            share = 1
            while (r * share) % steps or (r * share // steps) % BF16_ROWS:
                share *= 2
            rows = r * share // steps
            self.args.append(w)
            self.in_specs.append(pl.BlockSpec(
                (None, rows, c), lambda *g, idx=idx, share=share: (idx, step_of(*g) // share, 0)))
            self.out_specs.append(pl.BlockSpec(
                (rows, c), lambda *g, share=share: (step_of(*g) // share, 0)))
            self.out_shapes.append(jax.ShapeDtypeStruct((r, c), BF16))

    def __len__(self):
        return len(self.args)


def _run_side_casts(src_refs, dst_refs):
    for src, dst in zip(src_refs, dst_refs):
        dst[...] = src[...].astype(BF16)


def _pw1_glu_kernel(x_ref, g_ref, wa_ref, wg_ref, ba_ref, bg_ref, u_ref, h_scr):
    @pl.when(pl.program_id(1) == 0)
    def _():
        h_scr[...] = _rms(x_ref[...], g_ref[...]).astype(BF16)

    h = h_scr[...]
    a = _dot(h, wa_ref[...]) + ba_ref[...]
    g = _dot(h, wg_ref[...]) + bg_ref[...]
    u_ref[...] = a * _sigmoid(g)


def _pw1_glu(x, g_pre, w, b):
    m_rows, d = x.shape
    tm, tn = ROW_TILE, 512
    nn = d // tn
    return pl.pallas_call(
        _pw1_glu_kernel,
        grid=(m_rows // tm, nn),
        in_specs=[
            pl.BlockSpec((tm, d), lambda m, n: (m, 0)),
            pl.BlockSpec((1, d), lambda m, n: (0, 0)),
            pl.BlockSpec((d, tn), lambda m, n: (0, n)),
            pl.BlockSpec((d, tn), lambda m, n: (0, n + nn)),
            pl.BlockSpec((1, tn), lambda m, n: (0, n)),
            pl.BlockSpec((1, tn), lambda m, n: (0, n + nn)),
        ],
        out_specs=pl.BlockSpec((tm, tn), lambda m, n: (m, n)),
        out_shape=jax.ShapeDtypeStruct((m_rows, d), F32),
        scratch_shapes=[pltpu.VMEM((tm, d), BF16)],
        compiler_params=_compiler_params(("parallel", "arbitrary")),
        name="pw1_glu",
    )(x, g_pre, w, w, b, b)


def _ln_silu(v, ln_g, ln_b):
    xc = v - jnp.mean(v, axis=-1, keepdims=True)
    var = jnp.mean(xc * xc, axis=-1, keepdims=True)
    y = xc * lax.rsqrt(var + LN_EPS) * ln_g + ln_b
    return y * _sigmoid(y)


def _conv_taps(ext, cv, w_ref, b_ref, lt, tt):
    first_tap = CONV_HALO - CONV_STATE
    block_rows = SUBLANES * CONV_ROW_STRIDE
    lanes = slice(lt * LANES, (lt + 1) * LANES)
    bias = jnp.broadcast_to(b_ref[:, lanes], (SUBLANES, LANES))
    for r0 in range(0, tt, block_rows):
        accs = [bias] * CONV_ROW_STRIDE
        window = {}
        for j in range(CONV_WIDTH):
            wj = jnp.broadcast_to(w_ref[j:j + 1, lanes], (SUBLANES, LANES))
            for s in range(CONV_ROW_STRIDE):
                if s + j not in window:
                    window[s + j] = ext[lt, pl.ds(r0 + first_tap + s + j, SUBLANES, stride=CONV_ROW_STRIDE), :]
                accs[s] = accs[s] + wj * window[s + j]
        for s in range(CONV_ROW_STRIDE):
            cv[lt, pl.ds(r0 + s, SUBLANES, stride=CONV_ROW_STRIDE), :] = accs[s]


def _conv_mixer_prompt_kernel(n_side, tiles_per_seq, *refs):
    x_ref, g_ref, w_ref, bpw_ref, wdw_ref, bdw_ref, lg_ref, lb_ref = refs[:8]
    side_in = refs[8:8 + n_side]
    y_ref, tail_ref = refs[8 + n_side:10 + n_side]
    side_out = refs[10 + n_side:10 + 2 * n_side]
    ext, cv = refs[10 + 2 * n_side:]
    _run_side_casts(side_in, side_out)
    tt, d = x_ref.shape
    nl = d // LANES
    nn = d // CONV_MIXER_COLS
    tiles_per_block = CONV_MIXER_COLS // LANES
    t = pl.program_id(0) % tiles_per_seq

    @pl.when(t == 0)
    def _():
        ext[:, 0:CONV_HALO, :] = jnp.zeros((nl, CONV_HALO, LANES), F32)

    @pl.when(t != 0)
    def _():
        ext[:, 0:CONV_HALO, :] = ext[:, tt:tt + CONV_HALO, :]

    h = _rms(x_ref[...], g_ref[...]).astype(BF16)
    for n in range(nn + 1):
        if n < nn:
            a_cols = slice(n * CONV_MIXER_COLS, (n + 1) * CONV_MIXER_COLS)
            g_cols = slice(d + n * CONV_MIXER_COLS, d + (n + 1) * CONV_MIXER_COLS)
            a = _dot(h, w_ref[:, a_cols]) + bpw_ref[:, a_cols]
            g = _dot(h, w_ref[:, g_cols]) + bpw_ref[:, g_cols]
            u = a * _sigmoid(g)
            for l in range(tiles_per_block):
                ext[n * tiles_per_block + l, CONV_HALO:CONV_HALO + tt, :] = u[:, l * LANES:(l + 1) * LANES]
        if n > 0:
            for l in range(tiles_per_block):
                _conv_taps(ext, cv, wdw_ref, bdw_ref, (n - 1) * tiles_per_block + l, tt)

    for lt in range(nl):
        tail_ref[:, lt * LANES:(lt + 1) * LANES] = ext[lt, tt:tt + CONV_HALO, :]
    v = cv[...]
    xc = v - jnp.sum(jnp.sum(v, axis=0), axis=-1, keepdims=True) / d
    var = jnp.sum(jnp.sum(xc * xc, axis=0), axis=-1, keepdims=True) / d
    inv = lax.rsqrt(var + LN_EPS)
    for lt in range(nl):
        cols = slice(lt * LANES, (lt + 1) * LANES)
        y = xc[lt] * inv * lg_ref[:, cols] + lb_ref[:, cols]
        y_ref[:, cols] = (y * _sigmoid(y)).astype(BF16)


def _conv_mixer_prompt(x, batch, g_pre, w_pw1, b_pw1, w_dw, b_dw, ln_g, ln_b, side_weights):
    m_rows, d = x.shape
    tt = CONV_TIME_TILE
    nt = m_rows // batch // tt
    vec = pl.BlockSpec((1, d), lambda i: (0, 0))
    side = _SideCasts(side_weights, m_rows // tt, lambda i: i)
    outs = pl.pallas_call(
        functools.partial(_conv_mixer_prompt_kernel, len(side), nt),
        grid=(m_rows // tt,),
        in_specs=[
            pl.BlockSpec((tt, d), lambda i: (i, 0)), vec,
            pl.BlockSpec((d, 2 * d), lambda i: (0, 0), pipeline_mode=pl.Buffered(1)),
            pl.BlockSpec((1, 2 * d), lambda i: (0, 0)),
            pl.BlockSpec((CONV_WIDTH, d), lambda i: (0, 0)),
            vec, vec, vec,
        ] + side.in_specs,
        out_specs=[pl.BlockSpec((tt, d), lambda i: (i, 0)),
                   pl.BlockSpec((CONV_HALO, d), lambda i: (i, 0))] + side.out_specs,
        out_shape=[jax.ShapeDtypeStruct((m_rows, d), BF16),
                   jax.ShapeDtypeStruct((m_rows // tt * CONV_HALO, d), F32)] + side.out_shapes,
        scratch_shapes=[pltpu.VMEM((d // LANES, CONV_HALO + tt, LANES), F32),
                        pltpu.VMEM((d // LANES, tt, LANES), F32)],
        compiler_params=_compiler_params(("arbitrary",)),
        name="conv_mixer_prompt",
    )(x, g_pre, w_pw1, b_pw1, w_dw, b_dw, ln_g, ln_b, *side.args)
    tails = outs[1].reshape(batch, nt, CONV_HALO, d)
    return outs[0], tails[:, -1, CONV_HALO - CONV_STATE:], list(outs[2:])


def _conv_sample_kernel(cache_ref, u_ref, w_ref, b_ref, lg_ref, lb_ref, y_ref, state_ref, cv):
    hist, nseq, d = cache_ref.shape
    t_new = u_ref.shape[0]
    state_ref[0:hist - t_new] = cache_ref[t_new:hist]
    state_ref[hist - t_new:hist] = u_ref[...]

    def history_row(r, seqs, lanes):
        if r < hist:
            return cache_ref[r, seqs, lanes]
        return u_ref[r - hist, seqs, lanes]

    def lane_tile(lt, carry):
        lanes = pl.ds(pl.multiple_of(lt * LANES, LANES), LANES)
        bias = jnp.broadcast_to(b_ref[:, lanes], (SUBLANES, LANES))
        for s0 in range(0, nseq, SUBLANES):
            seqs = slice(s0, s0 + SUBLANES)
            accs = [bias] * t_new
            for j in range(CONV_WIDTH):
                wj = jnp.broadcast_to(w_ref[j:j + 1, lanes], (SUBLANES, LANES))
                for t in range(t_new):
                    accs[t] = accs[t] + wj * history_row(t + j, seqs, lanes)
            for t in range(t_new):
                cv[t, seqs, lanes] = accs[t]
        return carry

    lax.fori_loop(0, d // LANES, lane_tile, 0)
    y_ref[...] = _ln_silu(cv[...], lg_ref[...], lb_ref[...]).astype(BF16)


def _conv_sample(cache_t, u_t, w_dw, b_dw, ln_g, ln_b):
    hist, nb, d = cache_t.shape
    t_new = u_t.shape[0]
    nseq = SAMPLE_CONV_SEQS
    vec = pl.BlockSpec((1, d), lambda i: (0, 0))
    seq_block = lambda rows: pl.BlockSpec((rows, nseq, d), lambda i: (0, i, 0))
    return pl.pallas_call(
        _conv_sample_kernel,
        grid=(nb // nseq,),
        in_specs=[
            seq_block(hist), seq_block(t_new),
            pl.BlockSpec((CONV_WIDTH, d), lambda i: (0, 0)),
            vec, vec, vec,
        ],
        out_specs=[seq_block(t_new), seq_block(hist)],
        out_shape=[jax.ShapeDtypeStruct((t_new, nb, d), BF16),
                   jax.ShapeDtypeStruct((hist, nb, d), F32)],
        scratch_shapes=[pltpu.VMEM((t_new, nseq, d), F32)],
        compiler_params=_compiler_params(("parallel",)),
        name="conv_sample",
    )(cache_t, u_t, w_dw, b_dw, ln_g, ln_b)


def _proj_post_kernel(has_bias, *refs):
    if has_bias:
        a_ref, w_ref, b_ref, x_ref, gpost_ref, gnext_ref, xo_ref, ho_ref = refs
    else:
        a_ref, w_ref, x_ref, gpost_ref, gnext_ref, xo_ref, ho_ref = refs
    w = w_ref[...]
    sub = a_ref.shape[0] // PROJ_SUBTILES

    def finish(i, mix):
        rows = slice(i * sub, (i + 1) * sub)
        if has_bias:
            mix = mix + b_ref[...]
        x1 = x_ref[rows, :] + _rms(mix, gpost_ref[...])
        xo_ref[rows, :] = x1
        ho_ref[rows, :] = _rms(x1, gnext_ref[...]).astype(BF16)

    pending = None
    for i in range(PROJ_SUBTILES):
        mix = _dot(a_ref[i * sub:(i + 1) * sub, :], w)
        if pending is not None:
            finish(i - 1, pending)
        pending = mix
    finish(PROJ_SUBTILES - 1, pending)


def _proj_post(a, w, b, x, g_post, g_next):
    m_rows, k = a.shape
    d = w.shape[1]
    tm = ROW_TILE_FULL_N
    row = lambda width: pl.BlockSpec((tm, width), lambda m: (m, 0))
    vec = pl.BlockSpec((1, d), lambda m: (0, 0))
    has_bias = b is not None
    in_specs = [row(k), pl.BlockSpec((k, d), lambda m: (0, 0))]
    args = [a, w]
    if has_bias:
        in_specs.append(vec)
        args.append(b)
    in_specs += [row(d), vec, vec]
    args += [x, g_post, g_next]
    return pl.pallas_call(
        functools.partial(_proj_post_kernel, has_bias),
        grid=(m_rows // tm,),
        in_specs=in_specs,
        out_specs=[row(d), row(d)],
        out_shape=[jax.ShapeDtypeStruct((m_rows, d), F32), jax.ShapeDtypeStruct((m_rows, d), BF16)],
        compiler_params=_compiler_params(("parallel",)),
        name="proj_post",
    )(*args)


def _ffn_kernel(has_next, *refs):
    if has_next:
        h_ref, wg_ref, wu_ref, wd_ref, x_hbm, gpost_ref, gnext_ref, xo_ref, ho_ref, x_buf, x_sem = refs
    else:
        h_ref, wg_ref, wu_ref, wd_ref, x_hbm, gpost_ref, xo_ref, x_buf, x_sem = refs
    f = pl.program_id(1)
    tm = xo_ref.shape[0]
    row0 = pl.multiple_of(pl.program_id(0) * tm, tm)
    x_copy = pltpu.make_async_copy(x_hbm.at[pl.ds(row0, tm), :], x_buf, x_sem)

    @pl.when(f == 0)
    def _():
        x_copy.start()
        xo_ref[...] = jnp.zeros(xo_ref.shape, F32)

    h = h_ref[...]
    gate = _dot(h, wg_ref[...])
    up = _dot(h, wu_ref[...])
    xo_ref[...] += _dot((gate * _sigmoid(gate) * up).astype(BF16), wd_ref[...])

    @pl.when(f == pl.num_programs(1) - 1)
    def _():
        x_copy.wait()
        for r0 in range(0, tm, FFN_EPILOGUE_ROWS):
            rows = slice(r0, r0 + FFN_EPILOGUE_ROWS)
            x2 = x_buf[rows, :] + _rms(xo_ref[rows, :], gpost_ref[...])
            xo_ref[rows, :] = x2
            if has_next:
                ho_ref[rows, :] = _rms(x2, gnext_ref[...]).astype(BF16)


def _ffn(h, w_gate, w_up, w_down, x, g_post, g_next):
    m_rows, d = h.shape
    d_ff = w_gate.shape[1]
    tm, tf = FFN_ROW_TILE, 512
    row = pl.BlockSpec((tm, d), lambda m, f: (m, 0))
    vec = pl.BlockSpec((1, d), lambda m, f: (0, 0))
    has_next = g_next is not None
    in_specs = [
        pl.BlockSpec((tm, d), lambda m, f: (m, 0), pipeline_mode=pl.Buffered(1)),
        pl.BlockSpec((d, tf), lambda m, f: (0, f)),
        pl.BlockSpec((d, tf), lambda m, f: (0, f)),
        pl.BlockSpec((tf, d), lambda m, f: (f, 0)),
        pl.BlockSpec(memory_space=pl.ANY), vec,
    ]
    args = [h, w_gate, w_up, w_down, x, g_post]
    out_specs = [row]
    out_shape = [jax.ShapeDtypeStruct((m_rows, d), F32)]
    if has_next:
        in_specs.append(vec)
        args.append(g_next)
        out_specs.append(row)
        out_shape.append(jax.ShapeDtypeStruct((m_rows, d), BF16))
    outs = pl.pallas_call(
        functools.partial(_ffn_kernel, has_next),
        grid=(m_rows // tm, d_ff // tf),
        in_specs=in_specs,
        out_specs=out_specs,
        out_shape=out_shape,
        scratch_shapes=[pltpu.VMEM((tm, d), F32), pltpu.SemaphoreType.DMA(())],
        compiler_params=pltpu.CompilerParams(
            dimension_semantics=("parallel", "arbitrary"), vmem_limit_bytes=FFN_VMEM_LIMIT),
        name="ffn",
    )(*args)
    return (outs[0], outs[1]) if has_next else (outs[0], None)


def _qkvo_kernel(h_ref, wq_ref, wk_ref, wv_ref, wo_ref, wif_ref, bif_ref,
                 q_ref, k_ref, v_ref, o_ref, g_ref):
    h = h_ref[...]
    q_ref[...] = (_dot(h, wq_ref[...]) * (DQK_HEAD ** -0.5)).astype(BF16)
    k_ref[...] = _dot(h, wk_ref[...]).astype(BF16)
    v_ref[...] = _dot(h, wv_ref[...]).astype(BF16)
    o_ref[...] = _sigmoid(_dot(h, wo_ref[...]))

    @pl.when(pl.program_id(1) == 0)
    def _():
        z = _dot(h, wif_ref[...]) + bif_ref[...]
        lane = lax.broadcasted_iota(jnp.int32, z.shape, 1)
        log_sig = jnp.minimum(z, 0.0) - jnp.log1p(jnp.exp(-jnp.abs(z)))
        g_ref[...] = jnp.where(lane < N_HEADS, z, log_sig)


def _qkvo(h, w_q, w_k, w_v, w_o, w_if, b_if):
    m_rows, d = h.shape
    hk, hv = w_q.shape[1], w_v.shape[1]
    tm, steps = ROW_TILE, 4
    col = lambda width: pl.BlockSpec((d, width), lambda m, n: (0, n))
    out = lambda width: pl.BlockSpec((tm, width), lambda m, n: (m, n))
    return pl.pallas_call(
        _qkvo_kernel,
        grid=(m_rows // tm, steps),
        in_specs=[
            pl.BlockSpec((tm, d), lambda m, n: (m, 0)),
            col(hk // steps), col(hk // steps), col(hv // steps), col(hv // steps),
            pl.BlockSpec((d, LANES), lambda m, n: (0, 0)),
            pl.BlockSpec((1, LANES), lambda m, n: (0, 0)),
        ],
        out_specs=[out(hk // steps), out(hk // steps), out(hv // steps), out(hv // steps),
                   pl.BlockSpec((tm, LANES), lambda m, n: (m, 0))],
        out_shape=[
            jax.ShapeDtypeStruct((m_rows, hk), BF16),
            jax.ShapeDtypeStruct((m_rows, hk), BF16),
            jax.ShapeDtypeStruct((m_rows, hv), BF16),
            jax.ShapeDtypeStruct((m_rows, hv), F32),
            jax.ShapeDtypeStruct((m_rows, LANES), F32),
        ],
        compiler_params=_compiler_params(("parallel", "arbitrary")),
        name="qkvo",
    )(h, w_q, w_k, w_v, w_o, w_if, b_if)


def _cell_chunks(q, k, v, log_i, log_f, c_state, n_state, m_state):
    probs = range(len(q))
    length = q[0].shape[0]
    t_idx = lax.broadcasted_iota(jnp.int32, (length, length), 0)
    s_idx = lax.broadcasted_iota(jnp.int32, (length, length), 1)
    causal = s_idx <= t_idx
    eye = s_idx == t_idx
    nt_dims = (((1,), (1,)), ((), ()))
    tn_dims = (((0,), (0,)), ((), ()))

    qk = [lax.dot_general(q[p], k[p], nt_dims, preferred_element_type=F32) for p in probs]
    inter = [_dot(q[p], c_state[p].astype(BF16)) for p in probs]
    b_col = [jnp.sum(jnp.where(causal, log_f[p], 0.0), axis=1, keepdims=True) for p in probs]
    b_row = [jnp.sum(jnp.where(eye, b_col[p], 0.0), axis=0, keepdims=True) for p in probs]
    i_col = [jnp.sum(jnp.where(eye, log_i[p], 0.0), axis=1, keepdims=True) for p in probs]
    a_inter = [b_col[p] + m_state[p] for p in probs]
    d_intra = [jnp.where(causal, b_col[p] - b_row[p] + log_i[p], -jnp.inf) for p in probs]
    m_t = [jnp.maximum(a_inter[p], jnp.max(d_intra[p], axis=1, keepdims=True)) for p in probs]
    w_inter = [jnp.exp(a_inter[p] - m_t[p]) for p in probs]
    s = [qk[p] * jnp.exp(d_intra[p] - m_t[p]) for p in probs]
    intra = [_dot(s[p].astype(BF16), v[p]) for p in probs]

    def den(p):
        qn = w_inter[p] * (q[p].astype(F32) * n_state[p])
        if length % LANES == 0:
            for c0 in range(0, length, LANES):
                qn = qn + s[p][:, c0:c0 + LANES]
            return jnp.sum(qn, axis=1, keepdims=True)
        return jnp.sum(qn, axis=1, keepdims=True) + jnp.sum(s[p], axis=1, keepdims=True)

    h = [(w_inter[p] * inter[p] + intra[p]) / jnp.maximum(jnp.abs(den(p)), jnp.exp(-m_t[p]))
         for p in probs]

    m_new = [m_t[p][length - 1:length, :] for p in probs]
    b_last = [b_col[p][length - 1:length, :] for p in probs]
    g_state = [jnp.exp(b_last[p] + m_state[p] - m_new[p]) for p in probs]
    kw = [k[p].astype(F32) * jnp.exp(b_last[p] - b_col[p] + i_col[p] - m_new[p]) for p in probs]
    c_new = [g_state[p] * c_state[p]
             + lax.dot_general(kw[p].astype(BF16), v[p], tn_dims, preferred_element_type=F32)
             for p in probs]
    n_new = [g_state[p] * n_state[p] + jnp.sum(kw[p], axis=0, keepdims=True) for p in probs]
    return h, c_new, n_new, m_new


def _head_norm_gate(h, o, g_mh):
    hn = h * lax.rsqrt(jnp.mean(h * h, axis=-1, keepdims=True) + RMS_EPS)
    return o * (hn * g_mh)


def _cell_prompt_kernel(n_side, *refs):
    q_ref, k_ref, v_ref, o_ref, gt_ref, gmh_ref = refs[:6]
    side_in = refs[6:6 + n_side]
    hg_ref, c_out, n_out, m_out = refs[6 + n_side:10 + n_side]
    side_out = refs[10 + n_side:10 + 2 * n_side]
    c_scr, n_scr, m_scr = refs[10 + 2 * n_side:]
    _run_side_casts(side_in, side_out)
    c = pl.program_id(1)

    @pl.when(c == 0)
    def _():
        c_scr[...] = jnp.zeros(c_scr.shape, F32)
        n_scr[...] = jnp.zeros(n_scr.shape, F32)
        m_scr[...] = jnp.zeros(m_scr.shape, F32)

    heads = range(N_HEADS)
    qk_cols = [slice(hd * DQK_HEAD, (hd + 1) * DQK_HEAD) for hd in heads]
    v_cols = [slice(hd * DV_HEAD, (hd + 1) * DV_HEAD) for hd in heads]
    h, c_new, n_new, m_new = _cell_chunks(
        [q_ref[:, qk_cols[hd]] for hd in heads], [k_ref[:, qk_cols[hd]] for hd in heads],
        [v_ref[:, v_cols[hd]] for hd in heads],
        [gt_ref[hd:hd + 1, :] for hd in heads],
        [gt_ref[N_HEADS + hd:N_HEADS + hd + 1, :] for hd in heads],
        [c_scr[hd] for hd in heads], [n_scr[hd:hd + 1, :] for hd in heads],
        [m_scr[hd:hd + 1, 0:1] for hd in heads])
    for hd in heads:
        c_scr[hd] = c_new[hd]
        n_scr[hd:hd + 1, :] = n_new[hd]
        m_scr[hd:hd + 1, :] = jnp.broadcast_to(m_new[hd], (1, LANES))
        hg_ref[:, v_cols[hd]] = _head_norm_gate(
            h[hd], o_ref[:, v_cols[hd]], gmh_ref[:, v_cols[hd]]).astype(BF16)

    @pl.when(c == pl.num_programs(1) - 1)
    def _():
        c_out[0] = c_scr[...]
        n_out[0] = n_scr[...]
        m_out[0] = m_scr[...]


def _cell_prompt(q, k, v, o, gates_t, g_mh, batch, side_weights):
    m_rows, hk = q.shape
    hv = v.shape[1]
    length = CELL_CHUNK
    nc = m_rows // batch // length
    rows = lambda width: pl.BlockSpec((length, width), lambda b, c: (b * nc + c, 0))
    side = _SideCasts(side_weights, batch * nc, lambda b, c: b * nc + c)
    outs = pl.pallas_call(
        functools.partial(_cell_prompt_kernel, len(side)),
        grid=(batch, nc),
        in_specs=[
            rows(hk), rows(hk), rows(hv), rows(hv),
            pl.BlockSpec((2 * N_HEADS, length), lambda b, c: (0, b * nc + c)),
            pl.BlockSpec((1, hv), lambda b, c: (0, 0)),
        ] + side.in_specs,
        out_specs=[
            rows(hv),
            pl.BlockSpec((1, N_HEADS, DQK_HEAD, DV_HEAD), lambda b, c: (b, 0, 0, 0)),
            pl.BlockSpec((1, N_HEADS, DQK_HEAD), lambda b, c: (b, 0, 0)),
            pl.BlockSpec((1, N_HEADS, LANES), lambda b, c: (b, 0, 0)),
        ] + side.out_specs,
        out_shape=[
            jax.ShapeDtypeStruct((m_rows, hv), BF16),
            jax.ShapeDtypeStruct((batch, N_HEADS, DQK_HEAD, DV_HEAD), F32),
            jax.ShapeDtypeStruct((batch, N_HEADS, DQK_HEAD), F32),
            jax.ShapeDtypeStruct((batch, N_HEADS, LANES), F32),
        ] + side.out_shapes,
        scratch_shapes=[
            pltpu.VMEM((N_HEADS, DQK_HEAD, DV_HEAD), F32),
            pltpu.VMEM((N_HEADS, DQK_HEAD), F32),
            pltpu.VMEM((N_HEADS, LANES), F32),
        ],
        compiler_params=_compiler_params(("parallel", "arbitrary")),
        name="cell_prompt",
    )(q, k, v, o, gates_t, g_mh, *side.args)
    return list(outs[:4]), list(outs[4:])


def _cell_sample_kernel(q_ref, k_ref, v_ref, o_ref, gt_ref, gmh_ref, c_in, n_in, m_in,
                        hg_ref, c_out, n_out, m_out):
    t_new = o_ref.shape[1]

    heads = range(N_HEADS)
    qk_cols = [slice(hd * DQK_HEAD, (hd + 1) * DQK_HEAD) for hd in heads]
    v_cols = [slice(hd * DV_HEAD, (hd + 1) * DV_HEAD) for hd in heads]

    def sequence_group(g, carry):
        probs = [(g * SAMPLE_UNROLL + u, hd) for u in range(SAMPLE_UNROLL) for hd in heads]
        h, c_new, n_new, m_new = _cell_chunks(
            [q_ref[b, :, qk_cols[hd]] for b, hd in probs], [k_ref[b, :, qk_cols[hd]] for b, hd in probs],
            [v_ref[b, :, v_cols[hd]] for b, hd in probs],
            [gt_ref[b, hd:hd + 1, :] for b, hd in probs],
            [gt_ref[b, N_HEADS + hd:N_HEADS + hd + 1, :] for b, hd in probs],
            [c_in[b, hd] for b, hd in probs], [n_in[b, hd:hd + 1, :] for b, hd in probs],
            [m_in[b, hd:hd + 1, 0:1] for b, hd in probs])
        for p, (b, hd) in enumerate(probs):
            c_out[b, hd] = c_new[p]
            n_out[b, hd:hd + 1, :] = n_new[p]
            m_out[b, hd:hd + 1, :] = jnp.broadcast_to(m_new[p], (1, LANES))
            hg_ref[b, :, v_cols[hd]] = _head_norm_gate(
                h[p][0:t_new, :], o_ref[b, :, v_cols[hd]], gmh_ref[:, v_cols[hd]])
        return carry

    lax.fori_loop(0, q_ref.shape[0] // SAMPLE_UNROLL, sequence_group, 0)


def _cell_sample(q, k, v, o, gates_t, g_mh, c0, n0, m0):
    nb, tp, hk = q.shape
    hv = v.shape[2]
    t_new = o.shape[1]
    nseq = SAMPLE_SEQS
    blk3 = lambda s1, s2: pl.BlockSpec((nseq, s1, s2), lambda i: (i, 0, 0))
    c_spec = pl.BlockSpec((nseq, N_HEADS, DQK_HEAD, DV_HEAD), lambda i: (i, 0, 0, 0))
    return pl.pallas_call(
        _cell_sample_kernel,
        grid=(nb // nseq,),
        in_specs=[
            blk3(tp, hk), blk3(tp, hk), blk3(tp, hv), blk3(t_new, hv),
            blk3(2 * N_HEADS, tp),
            pl.BlockSpec((1, hv), lambda i: (0, 0)),
            c_spec, blk3(N_HEADS, DQK_HEAD), blk3(N_HEADS, LANES),
        ],
        out_specs=[blk3(t_new, hv), c_spec, blk3(N_HEADS, DQK_HEAD), blk3(N_HEADS, LANES)],
        out_shape=[
            jax.ShapeDtypeStruct((nb, t_new, hv), F32),
            jax.ShapeDtypeStruct((nb, N_HEADS, DQK_HEAD, DV_HEAD), F32),
            jax.ShapeDtypeStruct((nb, N_HEADS, DQK_HEAD), F32),
            jax.ShapeDtypeStruct((nb, N_HEADS, LANES), F32),
        ],
        compiler_params=_compiler_params(("parallel",)),
        name="cell_sample",
    )(q, k, v, o, gates_t, g_mh, c0, n0, m0)


def _row(v):
    return v.reshape(1, -1).astype(F32)


def _swap_row_order(a, outer, inner):
    return jnp.swapaxes(a.reshape(outer, inner, -1), 0, 1).reshape(outer * inner, -1)


def _conv_layer(x, batch, past, p, side_weights):
    m_rows, d = x.shape
    if past is None:
        y, state, side = _conv_mixer_prompt(
            x, batch, p["g_pre"], p["w_pw1"], p["b_pw1"], p["w_dw"], p["b_dw"], p["ln_g"], p["ln_b"],
            side_weights)
    else:
        side = []
        u = _pw1_glu(x, p["g_pre"], p["w_pw1"], p["b_pw1"])
        y_t, state_t = _conv_sample(
            jnp.swapaxes(past.astype(F32), 0, 1), u.reshape(m_rows // batch, batch, d),
            p["w_dw"], p["b_dw"], p["ln_g"], p["ln_b"])
        y = y_t.reshape(m_rows, d)
        state = jnp.swapaxes(state_t, 0, 1)
    return y, state, side


def _mlstm_layer(h, batch, state, p, side_weights):
    m_rows = h.shape[0]
    t_len = m_rows // batch
    q, k, v, o, gates = _qkvo(h, p["w_q"], p["w_k"], p["w_v"], p["w_o"], p["w_if"], p["b_if"])
    gates = gates[:, :2 * N_HEADS]
    side = []
    if state is None:
        (hg, c_new, n_new, m_new), side = _cell_prompt(
            q, k, v, o, gates.T, p["g_mh"], batch, side_weights)
    else:
        c0, n0, m0 = state
        pad_t = SAMPLE_PAD_T - t_len
        pad3 = lambda a: jnp.pad(a.reshape(batch, t_len, -1), ((0, 0), (0, pad_t), (0, 0)))
        g3 = gates.reshape(batch, t_len, 2 * N_HEADS)
        g_pad = jnp.concatenate(
            [jnp.full((batch, pad_t, N_HEADS), -jnp.inf, F32), jnp.zeros((batch, pad_t, N_HEADS), F32)], axis=2)
        gates_t = jnp.swapaxes(jnp.concatenate([g3, g_pad], axis=1), 1, 2)
        m0_lanes = jnp.broadcast_to(m0.astype(F32)[:, :, None], (batch, N_HEADS, LANES))
        hg, c_new, n_new, m_new = _cell_sample(
            pad3(q), pad3(k), pad3(v), o.reshape(batch, t_len, -1), gates_t, p["g_mh"],
            c0.astype(F32), n0.astype(F32), m0_lanes)
        hg = hg.reshape(m_rows, -1).astype(BF16)
    return hg, (c_new, n_new, m_new[:, :, 0]), side


def kernel(x_prompt, x_sample, cache_conv, state_C, state_n, state_m, conv_w_pw1, conv_b_pw1, conv_w_dw, conv_b_dw, conv_ln_g, conv_ln_b, conv_w_pw2, conv_b_pw2, mlstm_w_q, mlstm_w_k, mlstm_w_v, mlstm_w_o, mlstm_w_i, mlstm_b_i, mlstm_w_f, mlstm_b_f, mlstm_g_norm, mlstm_w_out, ffn_w_gate, ffn_w_up, ffn_w_down, norm_mix_pre, norm_mix_post, norm_ffn_pre, norm_ffn_post):
    depth = ffn_w_gate.shape[0]
    d = x_prompt.shape[-1]
    batches = (x_prompt.shape[0], x_sample.shape[0])
    xs = [x_prompt.reshape(-1, d), x_sample.reshape(-1, d)]
    hs = [None, None]
    conv_new = [[], []]
    c_new, n_new, m_new = [[], []], [[], []], [[], []]
    sample_batch, sample_t = x_sample.shape[0], x_sample.shape[1]
    sample_time_major = False
    next_own = None

    for i in range(depth):
        j = i // 2
        is_conv = i % 2 == 0
        if is_conv:
            p = {
                "g_pre": _row(norm_mix_pre[i]),
                "w_pw1": next_own[0] if next_own else conv_w_pw1[j].astype(BF16),
                "b_pw1": _row(conv_b_pw1[j]),
                "w_dw": conv_w_dw[j].astype(F32), "b_dw": _row(conv_b_dw[j]),
                "ln_g": _row(conv_ln_g[j]), "ln_b": _row(conv_ln_b[j]),
            }
            w_proj_all, b_proj = conv_w_pw2, _row(conv_b_pw2[j])
        else:
            w_if = jnp.concatenate([mlstm_w_i[j], mlstm_w_f[j]], axis=1)
            b_if = jnp.concatenate([mlstm_b_i[j], mlstm_b_f[j]], axis=0)
            pad = LANES - w_if.shape[1]
            own = next_own or [w[j].astype(BF16) for w in (mlstm_w_q, mlstm_w_k, mlstm_w_v, mlstm_w_o)]
            p = {
                "w_q": own[0], "w_k": own[1], "w_v": own[2], "w_o": own[3],
                "w_if": jnp.pad(w_if, ((0, 0), (0, pad))).astype(BF16),
                "b_if": jnp.pad(_row(b_if), ((0, 0), (0, pad))),
                "g_mh": _row(mlstm_g_norm[j]),
            }
            w_proj_all, b_proj = mlstm_w_out, None
        next_is_mlstm = i + 1 < depth and (i + 1) % 2 == 1
        g_next = _row(norm_mix_pre[i + 1]) if next_is_mlstm else None

        if sample_time_major != is_conv:
            dims = (sample_t, sample_batch) if sample_time_major else (sample_batch, sample_t)
            xs[1] = _swap_row_order(xs[1], *dims)
            hs[1] = None if hs[1] is None else _swap_row_order(hs[1], *dims)
            sample_time_major = is_conv

        to_cast = [(ffn_w_gate, i), (ffn_w_up, i), (ffn_w_down, i), (w_proj_all, j)]
        if i + 1 < depth:
            jn = (i + 1) // 2
            to_cast += ([(conv_w_pw1, jn)] if (i + 1) % 2 == 0 else
                        [(mlstm_w_q, jn), (mlstm_w_k, jn), (mlstm_w_v, jn), (mlstm_w_o, jn)])

        for gi in range(2):
            x, batch = xs[gi], batches[gi]
            side_weights = to_cast if gi == 0 else ()
            if is_conv:
                past = None if gi == 0 else cache_conv[j]
                mix_in, state, side = _conv_layer(x, batch, past, p, side_weights)
                conv_new[gi].append(state.astype(cache_conv.dtype))
            else:
                state = None if gi == 0 else (state_C[j], state_n[j], state_m[j])
                mix_in, (c1, n1, m1), side = _mlstm_layer(hs[gi], batch, state, p, side_weights)
                c_new[gi].append(c1.astype(state_C.dtype))
                n_new[gi].append(n1.astype(state_n.dtype))
                m_new[gi].append(m1.astype(state_m.dtype))
            if gi == 0:
                w_gate, w_up, w_down, w_proj = side[:4]
                next_own = side[4:]
            x, h_ffn = _proj_post(mix_in, w_proj, b_proj, x, _row(norm_mix_post[i]), _row(norm_ffn_pre[i]))
            xs[gi], hs[gi] = _ffn(h_ffn, w_gate, w_up, w_down, x, _row(norm_ffn_post[i]), g_next)

    if sample_time_major:
        xs[1] = _swap_row_order(xs[1], sample_t, sample_batch)

    return (xs[0].reshape(x_prompt.shape), xs[1].reshape(x_sample.shape),
            jnp.stack(conv_new[0]), jnp.stack(conv_new[1]),
            jnp.stack(c_new[0]), jnp.stack(n_new[0]), jnp.stack(m_new[0]),
            jnp.stack(c_new[1]), jnp.stack(n_new[1]), jnp.stack(m_new[1]))
```

```python
import functools

import jax
import jax.numpy as jnp
from jax import lax
from jax.experimental import pallas as pl
from jax.experimental.pallas import tpu as pltpu

F32 = jnp.float32
BF16 = jnp.bfloat16

D_MODEL = 2048
N_HEADS = 8
DV_HEAD = D_MODEL // N_HEADS
DQK_HEAD = DV_HEAD // 2
CONV_WIDTH = 31
CONV_STATE = CONV_WIDTH - 1
RMS_EPS = 1e-6
LN_EPS = 1e-5

LANES = 128
SUBLANES = 8
BF16_ROWS = 16
MIB = 1024 * 1024
VMEM_LIMIT = 52 * MIB

ROW_TILE = 1024
ROW_TILE_FULL_N = 512
PROJ_SUBTILES = 4
FFN_ROW_TILE = 1024
FFN_EPILOGUE_ROWS = 128
FFN_VMEM_LIMIT = 60 * MIB
CONV_TIME_TILE = 256
CONV_MIXER_COLS = 256
CONV_HALO = 32
CONV_ROW_STRIDE = 4
CELL_CHUNK = 256
SAMPLE_PAD_T = BF16_ROWS
SAMPLE_SEQS = 8
SAMPLE_UNROLL = 2
SAMPLE_CONV_SEQS = 16


def _compiler_params(semantics):
    return pltpu.CompilerParams(dimension_semantics=semantics, vmem_limit_bytes=VMEM_LIMIT)


def _dot(a, b):
    return jnp.dot(a, b, preferred_element_type=F32)


def _rms(x, g):
    return x * lax.rsqrt(jnp.mean(x * x, axis=-1, keepdims=True) + RMS_EPS) * g


def _sigmoid(x):
    return jax.nn.sigmoid(x)


class _SideCasts:
    def __init__(self, weights, steps, step_of):
        self.args, self.in_specs, self.out_specs, self.out_shapes = [], [], [], []
        for w, idx in weights:
            _, r, c = w.shape
            share = 1
            while (r * share) % steps or (r * share // steps) % BF16_ROWS:
                share *= 2
            rows = r * share // steps
            self.args.append(w)
            self.in_specs.append(pl.BlockSpec(
                (None, rows, c), lambda *g, idx=idx, share=share: (idx, step_of(*g) // share, 0)))
            self.out_specs.append(pl.BlockSpec(
                (rows, c), lambda *g, share=share: (step_of(*g) // share, 0)))
            self.out_shapes.append(jax.ShapeDtypeStruct((r, c), BF16))

    def __len__(self):
        return len(self.args)


def _run_side_casts(src_refs, dst_refs):
    for src, dst in zip(src_refs, dst_refs):
        dst[...] = src[...].astype(BF16)


def _pw1_glu_kernel(x_ref, g_ref, wa_ref, wg_ref, ba_ref, bg_ref, u_ref, h_scr):
    @pl.when(pl.program_id(1) == 0)
    def _():
        h_scr[...] = _rms(x_ref[...], g_ref[...]).astype(BF16)

    h = h_scr[...]
    a = _dot(h, wa_ref[...]) + ba_ref[...]
    g = _dot(h, wg_ref[...]) + bg_ref[...]
    u_ref[...] = a * _sigmoid(g)


def _pw1_glu(x, g_pre, w, b):
    m_rows, d = x.shape
    tm, tn = ROW_TILE, 512
    nn = d // tn
    return pl.pallas_call(
        _pw1_glu_kernel,
        grid=(m_rows // tm, nn),
        in_specs=[
            pl.BlockSpec((tm, d), lambda m, n: (m, 0)),
            pl.BlockSpec((1, d), lambda m, n: (0, 0)),
            pl.BlockSpec((d, tn), lambda m, n: (0, n)),
            pl.BlockSpec((d, tn), lambda m, n: (0, n + nn)),
            pl.BlockSpec((1, tn), lambda m, n: (0, n)),
            pl.BlockSpec((1, tn), lambda m, n: (0, n + nn)),
        ],
        out_specs=pl.BlockSpec((tm, tn), lambda m, n: (m, n)),
        out_shape=jax.ShapeDtypeStruct((m_rows, d), F32),
        scratch_shapes=[pltpu.VMEM((tm, d), BF16)],
        compiler_params=_compiler_params(("parallel", "arbitrary")),
        name="pw1_glu",
    )(x, g_pre, w, w, b, b)


def _ln_silu(v, ln_g, ln_b):
    xc = v - jnp.mean(v, axis=-1, keepdims=True)
    var = jnp.mean(xc * xc, axis=-1, keepdims=True)
    y = xc * lax.rsqrt(var + LN_EPS) * ln_g + ln_b
    return y * _sigmoid(y)


def _conv_taps(ext, cv, w_ref, b_ref, lt, tt):
    first_tap = CONV_HALO - CONV_STATE
    block_rows = SUBLANES * CONV_ROW_STRIDE
    lanes = slice(lt * LANES, (lt + 1) * LANES)
    bias = jnp.broadcast_to(b_ref[:, lanes], (SUBLANES, LANES))
    for r0 in range(0, tt, block_rows):
        accs = [bias] * CONV_ROW_STRIDE
        window = {}
        for j in range(CONV_WIDTH):
            wj = jnp.broadcast_to(w_ref[j:j + 1, lanes], (SUBLANES, LANES))
            for s in range(CONV_ROW_STRIDE):
                if s + j not in window:
                    window[s + j] = ext[lt, pl.ds(r0 + first_tap + s + j, SUBLANES, stride=CONV_ROW_STRIDE), :]
                accs[s] = accs[s] + wj * window[s + j]
        for s in range(CONV_ROW_STRIDE):
            cv[lt, pl.ds(r0 + s, SUBLANES, stride=CONV_ROW_STRIDE), :] = accs[s]


def _conv_mixer_prompt_kernel(n_side, tiles_per_seq, *refs):
    x_ref, g_ref, w_ref, bpw_ref, wdw_ref, bdw_ref, lg_ref, lb_ref = refs[:8]
    side_in = refs[8:8 + n_side]
    y_ref, tail_ref = refs[8 + n_side:10 + n_side]
    side_out = refs[10 + n_side:10 + 2 * n_side]
    ext, cv = refs[10 + 2 * n_side:]
    _run_side_casts(side_in, side_out)
    tt, d = x_ref.shape
    nl = d // LANES
    nn = d // CONV_MIXER_COLS
    tiles_per_block = CONV_MIXER_COLS // LANES
    t = pl.program_id(0) % tiles_per_seq

    @pl.when(t == 0)
    def _():
        ext[:, 0:CONV_HALO, :] = jnp.zeros((nl, CONV_HALO, LANES), F32)

    @pl.when(t != 0)
    def _():
        ext[:, 0:CONV_HALO, :] = ext[:, tt:tt + CONV_HALO, :]

    h = _rms(x_ref[...], g_ref[...]).astype(BF16)
    for n in range(nn + 1):
        if n < nn:
            a_cols = slice(n * CONV_MIXER_COLS, (n + 1) * CONV_MIXER_COLS)
            g_cols = slice(d + n * CONV_MIXER_COLS, d + (n + 1) * CONV_MIXER_COLS)
            a = _dot(h, w_ref[:, a_cols]) + bpw_ref[:, a_cols]
            g = _dot(h, w_ref[:, g_cols]) + bpw_ref[:, g_cols]
            u = a * _sigmoid(g)
            for l in range(tiles_per_block):
                ext[n * tiles_per_block + l, CONV_HALO:CONV_HALO + tt, :] = u[:, l * LANES:(l + 1) * LANES]
        if n > 0:
            for l in range(tiles_per_block):
                _conv_taps(ext, cv, wdw_ref, bdw_ref, (n - 1) * tiles_per_block + l, tt)

    for lt in range(nl):
        tail_ref[:, lt * LANES:(lt + 1) * LANES] = ext[lt, tt:tt + CONV_HALO, :]
    v = cv[...]
    xc = v - jnp.sum(jnp.sum(v, axis=0), axis=-1, keepdims=True) / d
    var = jnp.sum(jnp.sum(xc * xc, axis=0), axis=-1, keepdims=True) / d
    inv = lax.rsqrt(var + LN_EPS)
    for lt in range(nl):
        cols = slice(lt * LANES, (lt + 1) * LANES)
        y = xc[lt] * inv * lg_ref[:, cols] + lb_ref[:, cols]
        y_ref[:, cols] = (y * _sigmoid(y)).astype(BF16)


def _conv_mixer_prompt(x, batch, g_pre, w_pw1, b_pw1, w_dw, b_dw, ln_g, ln_b, side_weights):
    m_rows, d = x.shape
    tt = CONV_TIME_TILE
    nt = m_rows // batch // tt
    vec = pl.BlockSpec((1, d), lambda i: (0, 0))
    side = _SideCasts(side_weights, m_rows // tt, lambda i: i)
    outs = pl.pallas_call(
        functools.partial(_conv_mixer_prompt_kernel, len(side), nt),
        grid=(m_rows // tt,),
        in_specs=[
            pl.BlockSpec((tt, d), lambda i: (i, 0)), vec,
            pl.BlockSpec((d, 2 * d), lambda i: (0, 0), pipeline_mode=pl.Buffered(1)),
            pl.BlockSpec((1, 2 * d), lambda i: (0, 0)),
            pl.BlockSpec((CONV_WIDTH, d), lambda i: (0, 0)),
            vec, vec, vec,
        ] + side.in_specs,
        out_specs=[pl.BlockSpec((tt, d), lambda i: (i, 0)),
                   pl.BlockSpec((CONV_HALO, d), lambda i: (i, 0))] + side.out_specs,
        out_shape=[jax.ShapeDtypeStruct((m_rows, d), BF16),
                   jax.ShapeDtypeStruct((m_rows // tt * CONV_HALO, d), F32)] + side.out_shapes,
        scratch_shapes=[pltpu.VMEM((d // LANES, CONV_HALO + tt, LANES), F32),
                        pltpu.VMEM((d // LANES, tt, LANES), F32)],
        compiler_params=_compiler_params(("arbitrary",)),
        name="conv_mixer_prompt",
    )(x, g_pre, w_pw1, b_pw1, w_dw, b_dw, ln_g, ln_b, *side.args)
    tails = outs[1].reshape(batch, nt, CONV_HALO, d)
    return outs[0], tails[:, -1, CONV_HALO - CONV_STATE:], list(outs[2:])


def _conv_sample_kernel(cache_ref, u_ref, w_ref, b_ref, lg_ref, lb_ref, y_ref, state_ref, cv):
    hist, nseq, d = cache_ref.shape
    t_new = u_ref.shape[0]
    state_ref[0:hist - t_new] = cache_ref[t_new:hist]
    state_ref[hist - t_new:hist] = u_ref[...]

    def history_row(r, seqs, lanes):
        if r < hist:
            return cache_ref[r, seqs, lanes]
        return u_ref[r - hist, seqs, lanes]

    def lane_tile(lt, carry):
        lanes = pl.ds(pl.multiple_of(lt * LANES, LANES), LANES)
        bias = jnp.broadcast_to(b_ref[:, lanes], (SUBLANES, LANES))
        for s0 in range(0, nseq, SUBLANES):
            seqs = slice(s0, s0 + SUBLANES)
            accs = [bias] * t_new
            for j in range(CONV_WIDTH):
                wj = jnp.broadcast_to(w_ref[j:j + 1, lanes], (SUBLANES, LANES))
                for t in range(t_new):
                    accs[t] = accs[t] + wj * history_row(t + j, seqs, lanes)
            for t in range(t_new):
                cv[t, seqs, lanes] = accs[t]
        return carry

    lax.fori_loop(0, d // LANES, lane_tile, 0)
    y_ref[...] = _ln_silu(cv[...], lg_ref[...], lb_ref[...]).astype(BF16)


def _conv_sample(cache_t, u_t, w_dw, b_dw, ln_g, ln_b):
    hist, nb, d = cache_t.shape
    t_new = u_t.shape[0]
    nseq = SAMPLE_CONV_SEQS
    vec = pl.BlockSpec((1, d), lambda i: (0, 0))
    seq_block = lambda rows: pl.BlockSpec((rows, nseq, d), lambda i: (0, i, 0))
    return pl.pallas_call(
        _conv_sample_kernel,
        grid=(nb // nseq,),
        in_specs=[
            seq_block(hist), seq_block(t_new),
            pl.BlockSpec((CONV_WIDTH, d), lambda i: (0, 0)),
            vec, vec, vec,
        ],
        out_specs=[seq_block(t_new), seq_block(hist)],
        out_shape=[jax.ShapeDtypeStruct((t_new, nb, d), BF16),
                   jax.ShapeDtypeStruct((hist, nb, d), F32)],
        scratch_shapes=[pltpu.VMEM((t_new, nseq, d), F32)],
        compiler_params=_compiler_params(("parallel",)),
        name="conv_sample",
    )(cache_t, u_t, w_dw, b_dw, ln_g, ln_b)


def _proj_post_kernel(has_bias, *refs):
    if has_bias:
        a_ref, w_ref, b_ref, x_ref, gpost_ref, gnext_ref, xo_ref, ho_ref = refs
    else:
        a_ref, w_ref, x_ref, gpost_ref, gnext_ref, xo_ref, ho_ref = refs
    w = w_ref[...]
    sub = a_ref.shape[0] // PROJ_SUBTILES

    def finish(i, mix):
        rows = slice(i * sub, (i + 1) * sub)
        if has_bias:
            mix = mix + b_ref[...]
        x1 = x_ref[rows, :] + _rms(mix, gpost_ref[...])
        xo_ref[rows, :] = x1
        ho_ref[rows, :] = _rms(x1, gnext_ref[...]).astype(BF16)

    pending = None
    for i in range(PROJ_SUBTILES):
        mix = _dot(a_ref[i * sub:(i + 1) * sub, :], w)
        if pending is not None:
            finish(i - 1, pending)
        pending = mix
    finish(PROJ_SUBTILES - 1, pending)


def _proj_post(a, w, b, x, g_post, g_next):
    m_rows, k = a.shape
    d = w.shape[1]
    tm = ROW_TILE_FULL_N
    row = lambda width: pl.BlockSpec((tm, width), lambda m: (m, 0))
    vec = pl.BlockSpec((1, d), lambda m: (0, 0))
    has_bias = b is not None
    in_specs = [row(k), pl.BlockSpec((k, d), lambda m: (0, 0))]
    args = [a, w]
    if has_bias:
        in_specs.append(vec)
        args.append(b)
    in_specs += [row(d), vec, vec]
    args += [x, g_post, g_next]
    return pl.pallas_call(
        functools.partial(_proj_post_kernel, has_bias),
        grid=(m_rows // tm,),
        in_specs=in_specs,
        out_specs=[row(d), row(d)],
        out_shape=[jax.ShapeDtypeStruct((m_rows, d), F32), jax.ShapeDtypeStruct((m_rows, d), BF16)],
        compiler_params=_compiler_params(("parallel",)),
        name="proj_post",
    )(*args)


def _ffn_kernel(has_next, *refs):
    if has_next:
        h_ref, wg_ref, wu_ref, wd_ref, x_hbm, gpost_ref, gnext_ref, xo_ref, ho_ref, x_buf, x_sem = refs
    else:
        h_ref, wg_ref, wu_ref, wd_ref, x_hbm, gpost_ref, xo_ref, x_buf, x_sem = refs
    f = pl.program_id(1)
    tm = xo_ref.shape[0]
    row0 = pl.multiple_of(pl.program_id(0) * tm, tm)
    x_copy = pltpu.make_async_copy(x_hbm.at[pl.ds(row0, tm), :], x_buf, x_sem)

    @pl.when(f == 0)
    def _():
        x_copy.start()
        xo_ref[...] = jnp.zeros(xo_ref.shape, F32)

    h = h_ref[...]
    gate = _dot(h, wg_ref[...])
    up = _dot(h, wu_ref[...])
    xo_ref[...] += _dot((gate * _sigmoid(gate) * up).astype(BF16), wd_ref[...])

    @pl.when(f == pl.num_programs(1) - 1)
    def _():
        x_copy.wait()
        for r0 in range(0, tm, FFN_EPILOGUE_ROWS):
            rows = slice(r0, r0 + FFN_EPILOGUE_ROWS)
            x2 = x_buf[rows, :] + _rms(xo_ref[rows, :], gpost_ref[...])
            xo_ref[rows, :] = x2
            if has_next:
                ho_ref[rows, :] = _rms(x2, gnext_ref[...]).astype(BF16)


def _ffn(h, w_gate, w_up, w_down, x, g_post, g_next):
    m_rows, d = h.shape
    d_ff = w_gate.shape[1]
    tm, tf = FFN_ROW_TILE, 512
    row = pl.BlockSpec((tm, d), lambda m, f: (m, 0))
    vec = pl.BlockSpec((1, d), lambda m, f: (0, 0))
    has_next = g_next is not None
    in_specs = [
        row,
        pl.BlockSpec((d, tf), lambda m, f: (0, f)),
        pl.BlockSpec((d, tf), lambda m, f: (0, f)),
        pl.BlockSpec((tf, d), lambda m, f: (f, 0)),
        pl.BlockSpec(memory_space=pl.ANY), vec,
    ]
    args = [h, w_gate, w_up, w_down, x, g_post]
    out_specs = [row]
    out_shape = [jax.ShapeDtypeStruct((m_rows, d), F32)]
    if has_next:
        in_specs.append(vec)
        args.append(g_next)
        out_specs.append(row)
        out_shape.append(jax.ShapeDtypeStruct((m_rows, d), BF16))
    outs = pl.pallas_call(
        functools.partial(_ffn_kernel, has_next),
        grid=(m_rows // tm, d_ff // tf),
        in_specs=in_specs,
        out_specs=out_specs,
        out_shape=out_shape,
        scratch_shapes=[pltpu.VMEM((tm, d), F32), pltpu.SemaphoreType.DMA(())],
        compiler_params=pltpu.CompilerParams(
            dimension_semantics=("parallel", "arbitrary"), vmem_limit_bytes=FFN_VMEM_LIMIT),
        name="ffn",
    )(*args)
    return (outs[0], outs[1]) if has_next else (outs[0], None)


def _qkvo_kernel(h_ref, wq_ref, wk_ref, wv_ref, wo_ref, wif_ref, bif_ref,
                 q_ref, k_ref, v_ref, o_ref, g_ref):
    h = h_ref[...]
    q_ref[...] = (_dot(h, wq_ref[...]) * (DQK_HEAD ** -0.5)).astype(BF16)
    k_ref[...] = _dot(h, wk_ref[...]).astype(BF16)
    v_ref[...] = _dot(h, wv_ref[...]).astype(BF16)
    o_ref[...] = _sigmoid(_dot(h, wo_ref[...]))

    @pl.when(pl.program_id(1) == 0)
    def _():
        z = _dot(h, wif_ref[...]) + bif_ref[...]
        lane = lax.broadcasted_iota(jnp.int32, z.shape, 1)
        log_sig = jnp.minimum(z, 0.0) - jnp.log1p(jnp.exp(-jnp.abs(z)))
        g_ref[...] = jnp.where(lane < N_HEADS, z, log_sig)


def _qkvo(h, w_q, w_k, w_v, w_o, w_if, b_if):
    m_rows, d = h.shape
    hk, hv = w_q.shape[1], w_v.shape[1]
    tm, steps = ROW_TILE, 4
    col = lambda width: pl.BlockSpec((d, width), lambda m, n: (0, n))
    out = lambda width: pl.BlockSpec((tm, width), lambda m, n: (m, n))
    return pl.pallas_call(
        _qkvo_kernel,
        grid=(m_rows // tm, steps),
        in_specs=[
            pl.BlockSpec((tm, d), lambda m, n: (m, 0)),
            col(hk // steps), col(hk // steps), col(hv // steps), col(hv // steps),
            pl.BlockSpec((d, LANES), lambda m, n: (0, 0)),
            pl.BlockSpec((1, LANES), lambda m, n: (0, 0)),
        ],
        out_specs=[out(hk // steps), out(hk // steps), out(hv // steps), out(hv // steps),
                   pl.BlockSpec((tm, LANES), lambda m, n: (m, 0))],
        out_shape=[
            jax.ShapeDtypeStruct((m_rows, hk), BF16),
            jax.ShapeDtypeStruct((m_rows, hk), BF16),
            jax.ShapeDtypeStruct((m_rows, hv), BF16),
            jax.ShapeDtypeStruct((m_rows, hv), F32),
            jax.ShapeDtypeStruct((m_rows, LANES), F32),
        ],
        compiler_params=_compiler_params(("parallel", "arbitrary")),
        name="qkvo",
    )(h, w_q, w_k, w_v, w_o, w_if, b_if)


def _cell_chunks(q, k, v, log_i, log_f, c_state, n_state, m_state):
    probs = range(len(q))
    length = q[0].shape[0]
    t_idx = lax.broadcasted_iota(jnp.int32, (length, length), 0)
    s_idx = lax.broadcasted_iota(jnp.int32, (length, length), 1)
    causal = s_idx <= t_idx
    eye = s_idx == t_idx
    nt_dims = (((1,), (1,)), ((), ()))
    tn_dims = (((0,), (0,)), ((), ()))

    qk = [lax.dot_general(q[p], k[p], nt_dims, preferred_element_type=F32) for p in probs]
    inter = [_dot(q[p], c_state[p].astype(BF16)) for p in probs]
    b_col = [jnp.sum(jnp.where(causal, log_f[p], 0.0), axis=1, keepdims=True) for p in probs]
    b_row = [jnp.sum(jnp.where(eye, b_col[p], 0.0), axis=0, keepdims=True) for p in probs]
    i_col = [jnp.sum(jnp.where(eye, log_i[p], 0.0), axis=1, keepdims=True) for p in probs]
    a_inter = [b_col[p] + m_state[p] for p in probs]
    d_intra = [jnp.where(causal, b_col[p] - b_row[p] + log_i[p], -jnp.inf) for p in probs]
    m_t = [jnp.maximum(a_inter[p], jnp.max(d_intra[p], axis=1, keepdims=True)) for p in probs]
    w_inter = [jnp.exp(a_inter[p] - m_t[p]) for p in probs]
    s = [qk[p] * jnp.exp(d_intra[p] - m_t[p]) for p in probs]
    intra = [_dot(s[p].astype(BF16), v[p]) for p in probs]

    def den(p):
        qn = w_inter[p] * (q[p].astype(F32) * n_state[p])
        if length % LANES == 0:
            for c0 in range(0, length, LANES):
                qn = qn + s[p][:, c0:c0 + LANES]
            return jnp.sum(qn, axis=1, keepdims=True)
        return jnp.sum(qn, axis=1, keepdims=True) + jnp.sum(s[p], axis=1, keepdims=True)

    h = [(w_inter[p] * inter[p] + intra[p]) / jnp.maximum(jnp.abs(den(p)), jnp.exp(-m_t[p]))
         for p in probs]

    m_new = [m_t[p][length - 1:length, :] for p in probs]
    b_last = [b_col[p][length - 1:length, :] for p in probs]
    g_state = [jnp.exp(b_last[p] + m_state[p] - m_new[p]) for p in probs]
    kw = [k[p].astype(F32) * jnp.exp(b_last[p] - b_col[p] + i_col[p] - m_new[p]) for p in probs]
    c_new = [g_state[p] * c_state[p]
             + lax.dot_general(kw[p].astype(BF16), v[p], tn_dims, preferred_element_type=F32)
             for p in probs]
    n_new = [g_state[p] * n_state[p] + jnp.sum(kw[p], axis=0, keepdims=True) for p in probs]
    return h, c_new, n_new, m_new


def _head_norm_gate(h, o, g_mh):
    hn = h * lax.rsqrt(jnp.mean(h * h, axis=-1, keepdims=True) + RMS_EPS)
    return o * (hn * g_mh)


def _cell_prompt_kernel(n_side, *refs):
    q_ref, k_ref, v_ref, o_ref, gt_ref, gmh_ref = refs[:6]
    side_in = refs[6:6 + n_side]
    hg_ref, c_out, n_out, m_out = refs[6 + n_side:10 + n_side]
    side_out = refs[10 + n_side:10 + 2 * n_side]
    c_scr, n_scr, m_scr = refs[10 + 2 * n_side:]
    _run_side_casts(side_in, side_out)
    c = pl.program_id(1)

    @pl.when(c == 0)
    def _():
        c_scr[...] = jnp.zeros(c_scr.shape, F32)
        n_scr[...] = jnp.zeros(n_scr.shape, F32)
        m_scr[...] = jnp.zeros(m_scr.shape, F32)

    heads = range(N_HEADS)
    qk_cols = [slice(hd * DQK_HEAD, (hd + 1) * DQK_HEAD) for hd in heads]
    v_cols = [slice(hd * DV_HEAD, (hd + 1) * DV_HEAD) for hd in heads]
    h, c_new, n_new, m_new = _cell_chunks(
        [q_ref[:, qk_cols[hd]] for hd in heads], [k_ref[:, qk_cols[hd]] for hd in heads],
        [v_ref[:, v_cols[hd]] for hd in heads],
        [gt_ref[hd:hd + 1, :] for hd in heads],
        [gt_ref[N_HEADS + hd:N_HEADS + hd + 1, :] for hd in heads],
        [c_scr[hd] for hd in heads], [n_scr[hd:hd + 1, :] for hd in heads],
        [m_scr[hd:hd + 1, 0:1] for hd in heads])
    for hd in heads:
        c_scr[hd] = c_new[hd]
        n_scr[hd:hd + 1, :] = n_new[hd]
        m_scr[hd:hd + 1, :] = jnp.broadcast_to(m_new[hd], (1, LANES))
        hg_ref[:, v_cols[hd]] = _head_norm_gate(
            h[hd], o_ref[:, v_cols[hd]], gmh_ref[:, v_cols[hd]]).astype(BF16)

    @pl.when(c == pl.num_programs(1) - 1)
    def _():
        c_out[0] = c_scr[...]
        n_out[0] = n_scr[...]
        m_out[0] = m_scr[...]


def _cell_prompt(q, k, v, o, gates_t, g_mh, batch, side_weights):
    m_rows, hk = q.shape
    hv = v.shape[1]
    length = CELL_CHUNK
    nc = m_rows // batch // length
    rows = lambda width: pl.BlockSpec((length, width), lambda b, c: (b * nc + c, 0))
    side = _SideCasts(side_weights, batch * nc, lambda b, c: b * nc + c)
    outs = pl.pallas_call(
        functools.partial(_cell_prompt_kernel, len(side)),
        grid=(batch, nc),
        in_specs=[
            rows(hk), rows(hk), rows(hv), rows(hv),
            pl.BlockSpec((2 * N_HEADS, length), lambda b, c: (0, b * nc + c)),
            pl.BlockSpec((1, hv), lambda b, c: (0, 0)),
        ] + side.in_specs,
        out_specs=[
            rows(hv),
            pl.BlockSpec((1, N_HEADS, DQK_HEAD, DV_HEAD), lambda b, c: (b, 0, 0, 0)),
            pl.BlockSpec((1, N_HEADS, DQK_HEAD), lambda b, c: (b, 0, 0)),
            pl.BlockSpec((1, N_HEADS, LANES), lambda b, c: (b, 0, 0)),
        ] + side.out_specs,
        out_shape=[
            jax.ShapeDtypeStruct((m_rows, hv), BF16),
            jax.ShapeDtypeStruct((batch, N_HEADS, DQK_HEAD, DV_HEAD), F32),
            jax.ShapeDtypeStruct((batch, N_HEADS, DQK_HEAD), F32),
            jax.ShapeDtypeStruct((batch, N_HEADS, LANES), F32),
        ] + side.out_shapes,
        scratch_shapes=[
            pltpu.VMEM((N_HEADS, DQK_HEAD, DV_HEAD), F32),
            pltpu.VMEM((N_HEADS, DQK_HEAD), F32),
            pltpu.VMEM((N_HEADS, LANES), F32),
        ],
        compiler_params=_compiler_params(("parallel", "arbitrary")),
        name="cell_prompt",
    )(q, k, v, o, gates_t, g_mh, *side.args)
    return list(outs[:4]), list(outs[4:])


def _cell_sample_kernel(q_ref, k_ref, v_ref, o_ref, gt_ref, gmh_ref, c_in, n_in, m_in,
                        hg_ref, c_out, n_out, m_out):
    t_new = o_ref.shape[1]

    heads = range(N_HEADS)
    qk_cols = [slice(hd * DQK_HEAD, (hd + 1) * DQK_HEAD) for hd in heads]
    v_cols = [slice(hd * DV_HEAD, (hd + 1) * DV_HEAD) for hd in heads]

    def sequence_group(g, carry):
        probs = [(g * SAMPLE_UNROLL + u, hd) for u in range(SAMPLE_UNROLL) for hd in heads]
        h, c_new, n_new, m_new = _cell_chunks(
            [q_ref[b, :, qk_cols[hd]] for b, hd in probs], [k_ref[b, :, qk_cols[hd]] for b, hd in probs],
            [v_ref[b, :, v_cols[hd]] for b, hd in probs],
            [gt_ref[b, hd:hd + 1, :] for b, hd in probs],
            [gt_ref[b, N_HEADS + hd:N_HEADS + hd + 1, :] for b, hd in probs],
            [c_in[b, hd] for b, hd in probs], [n_in[b, hd:hd + 1, :] for b, hd in probs],
            [m_in[b, hd:hd + 1, 0:1] for b, hd in probs])
        for p, (b, hd) in enumerate(probs):
            c_out[b, hd] = c_new[p]
            n_out[b, hd:hd + 1, :] = n_new[p]
            m_out[b, hd:hd + 1, :] = jnp.broadcast_to(m_new[p], (1, LANES))
            hg_ref[b, :, v_cols[hd]] = _head_norm_gate(
                h[p][0:t_new, :], o_ref[b, :, v_cols[hd]], gmh_ref[:, v_cols[hd]])
        return carry

    lax.fori_loop(0, q_ref.shape[0] // SAMPLE_UNROLL, sequence_group, 0)


def _cell_sample(q, k, v, o, gates_t, g_mh, c0, n0, m0):
    nb, tp, hk = q.shape
    hv = v.shape[2]
    t_new = o.shape[1]
    nseq = SAMPLE_SEQS
    blk3 = lambda s1, s2: pl.BlockSpec((nseq, s1, s2), lambda i: (i, 0, 0))
    c_spec = pl.BlockSpec((nseq, N_HEADS, DQK_HEAD, DV_HEAD), lambda i: (i, 0, 0, 0))
    return pl.pallas_call(
        _cell_sample_kernel,
        grid=(nb // nseq,),
        in_specs=[
            blk3(tp, hk), blk3(tp, hk), blk3(tp, hv), blk3(t_new, hv),
            blk3(2 * N_HEADS, tp),
            pl.BlockSpec((1, hv), lambda i: (0, 0)),
            c_spec, blk3(N_HEADS, DQK_HEAD), blk3(N_HEADS, LANES),
        ],
        out_specs=[blk3(t_new, hv), c_spec, blk3(N_HEADS, DQK_HEAD), blk3(N_HEADS, LANES)],
        out_shape=[
            jax.ShapeDtypeStruct((nb, t_new, hv), F32),
            jax.ShapeDtypeStruct((nb, N_HEADS, DQK_HEAD, DV_HEAD), F32),
            jax.ShapeDtypeStruct((nb, N_HEADS, DQK_HEAD), F32),
            jax.ShapeDtypeStruct((nb, N_HEADS, LANES), F32),
        ],
        compiler_params=_compiler_params(("parallel",)),
        name="cell_sample",
    )(q, k, v, o, gates_t, g_mh, c0, n0, m0)


def _row(v):
    return v.reshape(1, -1).astype(F32)


def _swap_row_order(a, outer, inner):
    return jnp.swapaxes(a.reshape(outer, inner, -1), 0, 1).reshape(outer * inner, -1)


def _conv_layer(x, batch, past, p, side_weights):
    m_rows, d = x.shape
    if past is None:
        y, state, side = _conv_mixer_prompt(
            x, batch, p["g_pre"], p["w_pw1"], p["b_pw1"], p["w_dw"], p["b_dw"], p["ln_g"], p["ln_b"],
            side_weights)
    else:
        side = []
        u = _pw1_glu(x, p["g_pre"], p["w_pw1"], p["b_pw1"])
        y_t, state_t = _conv_sample(
            jnp.swapaxes(past.astype(F32), 0, 1), u.reshape(m_rows // batch, batch, d),
            p["w_dw"], p["b_dw"], p["ln_g"], p["ln_b"])
        y = y_t.reshape(m_rows, d)
        state = jnp.swapaxes(state_t, 0, 1)
    return y, state, side


def _mlstm_layer(h, batch, state, p, side_weights):
    m_rows = h.shape[0]
    t_len = m_rows // batch
    q, k, v, o, gates = _qkvo(h, p["w_q"], p["w_k"], p["w_v"], p["w_o"], p["w_if"], p["b_if"])
    gates = gates[:, :2 * N_HEADS]
    side = []
    if state is None:
        (hg, c_new, n_new, m_new), side = _cell_prompt(
            q, k, v, o, gates.T, p["g_mh"], batch, side_weights)
    else:
        c0, n0, m0 = state
        pad_t = SAMPLE_PAD_T - t_len
        pad3 = lambda a: jnp.pad(a.reshape(batch, t_len, -1), ((0, 0), (0, pad_t), (0, 0)))
        g3 = gates.reshape(batch, t_len, 2 * N_HEADS)
        g_pad = jnp.concatenate(
            [jnp.full((batch, pad_t, N_HEADS), -jnp.inf, F32), jnp.zeros((batch, pad_t, N_HEADS), F32)], axis=2)
        gates_t = jnp.swapaxes(jnp.concatenate([g3, g_pad], axis=1), 1, 2)
        m0_lanes = jnp.broadcast_to(m0.astype(F32)[:, :, None], (batch, N_HEADS, LANES))
        hg, c_new, n_new, m_new = _cell_sample(
            pad3(q), pad3(k), pad3(v), o.reshape(batch, t_len, -1), gates_t, p["g_mh"],
            c0.astype(F32), n0.astype(F32), m0_lanes)
        hg = hg.reshape(m_rows, -1).astype(BF16)
    return hg, (c_new, n_new, m_new[:, :, 0]), side


def kernel(x_prompt, x_sample, cache_conv, state_C, state_n, state_m, conv_w_pw1, conv_b_pw1, conv_w_dw, conv_b_dw, conv_ln_g, conv_ln_b, conv_w_pw2, conv_b_pw2, mlstm_w_q, mlstm_w_k, mlstm_w_v, mlstm_w_o, mlstm_w_i, mlstm_b_i, mlstm_w_f, mlstm_b_f, mlstm_g_norm, mlstm_w_out, ffn_w_gate, ffn_w_up, ffn_w_down, norm_mix_pre, norm_mix_post, norm_ffn_pre, norm_ffn_post):
    depth = ffn_w_gate.shape[0]
    d = x_prompt.shape[-1]
    batches = (x_prompt.shape[0], x_sample.shape[0])
    xs = [x_prompt.reshape(-1, d), x_sample.reshape(-1, d)]
    hs = [None, None]
    conv_new = [[], []]
    c_new, n_new, m_new = [[], []], [[], []], [[], []]
    sample_batch, sample_t = x_sample.shape[0], x_sample.shape[1]
    sample_time_major = False
    next_own = None

    for i in range(depth):
        j = i // 2
        is_conv = i % 2 == 0
        if is_conv:
            p = {
                "g_pre": _row(norm_mix_pre[i]),
                "w_pw1": next_own[0] if next_own else conv_w_pw1[j].astype(BF16),
                "b_pw1": _row(conv_b_pw1[j]),
                "w_dw": conv_w_dw[j].astype(F32), "b_dw": _row(conv_b_dw[j]),
                "ln_g": _row(conv_ln_g[j]), "ln_b": _row(conv_ln_b[j]),
            }
            w_proj_all, b_proj = conv_w_pw2, _row(conv_b_pw2[j])
        else:
            w_if = jnp.concatenate([mlstm_w_i[j], mlstm_w_f[j]], axis=1)
            b_if = jnp.concatenate([mlstm_b_i[j], mlstm_b_f[j]], axis=0)
            pad = LANES - w_if.shape[1]
            own = next_own or [w[j].astype(BF16) for w in (mlstm_w_q, mlstm_w_k, mlstm_w_v, mlstm_w_o)]
            p = {
                "w_q": own[0], "w_k": own[1], "w_v": own[2], "w_o": own[3],
                "w_if": jnp.pad(w_if, ((0, 0), (0, pad))).astype(BF16),
                "b_if": jnp.pad(_row(b_if), ((0, 0), (0, pad))),
                "g_mh": _row(mlstm_g_norm[j]),
            }
            w_proj_all, b_proj = mlstm_w_out, None
        next_is_mlstm = i + 1 < depth and (i + 1) % 2 == 1
        g_next = _row(norm_mix_pre[i + 1]) if next_is_mlstm else None

        if sample_time_major != is_conv:
            dims = (sample_t, sample_batch) if sample_time_major else (sample_batch, sample_t)
            xs[1] = _swap_row_order(xs[1], *dims)
            hs[1] = None if hs[1] is None else _swap_row_order(hs[1], *dims)
            sample_time_major = is_conv

        to_cast = [(ffn_w_gate, i), (ffn_w_up, i), (ffn_w_down, i), (w_proj_all, j)]
        if i + 1 < depth:
            jn = (i + 1) // 2
            to_cast += ([(conv_w_pw1, jn)] if (i + 1) % 2 == 0 else
                        [(mlstm_w_q, jn), (mlstm_w_k, jn), (mlstm_w_v, jn), (mlstm_w_o, jn)])

        for gi in range(2):
            x, batch = xs[gi], batches[gi]
            side_weights = to_cast if gi == 0 else ()
            if is_conv:
                past = None if gi == 0 else cache_conv[j]
                mix_in, state, side = _conv_layer(x, batch, past, p, side_weights)
                conv_new[gi].append(state.astype(cache_conv.dtype))
            else:
                state = None if gi == 0 else (state_C[j], state_n[j], state_m[j])
                mix_in, (c1, n1, m1), side = _mlstm_layer(hs[gi], batch, state, p, side_weights)
                c_new[gi].append(c1.astype(state_C.dtype))
                n_new[gi].append(n1.astype(state_n.dtype))
                m_new[gi].append(m1.astype(state_m.dtype))
            if gi == 0:
                w_gate, w_up, w_down, w_proj = side[:4]
                next_own = side[4:]
            x, h_ffn = _proj_post(mix_in, w_proj, b_proj, x, _row(norm_mix_post[i]), _row(norm_ffn_pre[i]))
            xs[gi], hs[gi] = _ffn(h_ffn, w_gate, w_up, w_down, x, _row(norm_ffn_post[i]), g_next)

    if sample_time_major:
        xs[1] = _swap_row_order(xs[1], sample_t, sample_batch)

    return (xs[0].reshape(x_prompt.shape), xs[1].reshape(x_sample.shape),
            jnp.stack(conv_new[0]), jnp.stack(conv_new[1]),
            jnp.stack(c_new[0]), jnp.stack(n_new[0]), jnp.stack(m_new[0]),
            jnp.stack(c_new[1]), jnp.stack(n_new[1]), jnp.stack(m_new[1]))
```
